```python
import math
import jax, jax.numpy as jnp
from jax import lax
import numpy as np

D_MODEL = 1024
BATCH = 8
SEQ = 8192
DEPTH = 1
DEC_BATCH = 16
DEC_SEQ = 32
PAST_LEN = 4096

CHUNK = 64
D_MIX = 2 * D_MODEL
HA = 8
DK_A = 128
DV_A = 128
CONV_W = 4
WA = HA * DV_A
QKV_A = HA * (2 * DK_A + DV_A)
HB = 4
DQK_B = 64
DV_B = 2 * DQK_B
WB = HB * DV_B
HC = 4
DH_C = 128
WC = HC * DH_C
N_MEM = 256
Q_BLOCK = 128
EPS = 1e-6
IN_SIZES = (QKV_A, WA, HA, HA, HB * 2 * DQK_B, HB * 2 * DQK_B, WB, WB, WC, WC)
N_IN = sum(IN_SIZES)

kernel_name = 'hybrid_gdn_diffattn_mem_stream_step'


def rmsnorm(x, g):
    xf = x.astype(jnp.float32)
    y = xf * lax.rsqrt(jnp.mean(xf * xf, -1, keepdims=True) + EPS)
    return (y * g.astype(jnp.float32)).astype(x.dtype)


def l2norm(x):
    xf = x.astype(jnp.float32)
    return xf * lax.rsqrt(jnp.sum(xf * xf, -1, keepdims=True) + EPS)


def split_points():
    pts, acc = [], 0
    for s in IN_SIZES[:-1]:
        acc += s
        pts.append(acc)
    return pts


def causal_conv(x, buf, w):
    L = x.shape[1]
    xp = jnp.concatenate([buf.astype(x.dtype), x], axis=1)
    y = xp[:, 0:L] * w[0]
    for i in range(1, CONV_W):
        y = y + xp[:, i:i + L] * w[i]
    return jax.nn.silu(y), xp[:, -(CONV_W - 1):]


def gated_delta_blocks(q, k, v, g, beta, s0):
    B, L, H, DK = q.shape
    DV = v.shape[-1]
    C = min(CHUNK, L)
    N = L // C
    qc = q.reshape(B, N, C, H, DK).transpose(0, 1, 3, 2, 4)
    kc = k.reshape(B, N, C, H, DK).transpose(0, 1, 3, 2, 4)
    vc = v.reshape(B, N, C, H, DV).transpose(0, 1, 3, 2, 4)
    gc = g.reshape(B, N, C, H).transpose(0, 1, 3, 2)
    bc = beta.reshape(B, N, C, H).transpose(0, 1, 3, 2)
    G = jnp.cumsum(gc, -1)
    causal = jnp.tril(jnp.ones((C, C), bool))
    strict = jnp.tril(jnp.ones((C, C), bool), -1)
    decay = jnp.exp(jnp.where(causal, G[..., :, None] - G[..., None, :], -jnp.inf))
    kk = jnp.einsum('bnhid,bnhjd->bnhij', kc, kc)
    lmat = jnp.where(strict, bc[..., :, None] * kk * decay, 0.0)
    eye = jnp.eye(C, dtype=jnp.float32)
    rhs = jnp.concatenate([vc * bc[..., None], kc * (bc * jnp.exp(G))[..., None]], -1)
    sol = lax.linalg.triangular_solve(eye + lmat, rhs, left_side=True, lower=True,
                                      unit_diagonal=True)
    U, W = sol[..., :DV], sol[..., DV:]
    attn = jnp.einsum('bnhid,bnhjd->bnhij', qc, kc) * decay
    q_dec = qc * jnp.exp(G)[..., None]
    k_tail = kc * jnp.exp(G[..., -1:] - G)[..., None]
    g_last = jnp.exp(G[..., -1])

    def step(S, xs):
        u_n, w_n, a_n, qd_n, kt_n, gl_n = xs
        e = u_n - jnp.einsum('bhcd,bhde->bhce', w_n, S)
        o = jnp.einsum('bhcd,bhde->bhce', qd_n, S) + jnp.einsum('bhij,bhje->bhie', a_n, e)
        S = S * gl_n[..., None, None] + jnp.einsum('bhcd,bhce->bhde', kt_n, e)
        return S, o

    xs = tuple(jnp.moveaxis(t, 1, 0) for t in (U, W, attn, q_dec, k_tail, g_last))
    S, o = lax.scan(step, s0.astype(jnp.float32), xs)
    o = o.transpose(1, 0, 3, 2, 4).reshape(B, L, H, DV)
    return o, S


def diff_attn_rows(q, qpos, k, v, kpos, lam):
    s = jnp.einsum('bqhmd,bkhmd->bhmqk', q, k).astype(jnp.float32) * DQK_B ** -0.5
    mask = (kpos[None, :] // CHUNK) <= (qpos[:, None] // CHUNK)
    p = jax.nn.softmax(jnp.where(mask, s, -jnp.inf), axis=-1)
    a = p[:, :, 0] - lam * p[:, :, 1]
    return jnp.einsum('bhqk,bkhe->bqhe', a.astype(v.dtype), v)


def diff_attend(q, qpos, k, v, kpos, lam):
    B, L = q.shape[:2]
    if L > Q_BLOCK and L % Q_BLOCK == 0:
        nb = L // Q_BLOCK
        qb = jnp.moveaxis(q.reshape(B, nb, Q_BLOCK, *q.shape[2:]), 1, 0)
        pb = qpos.reshape(nb, Q_BLOCK)
        o = lax.map(lambda xs: diff_attn_rows(xs[0], xs[1], k, v, kpos, lam), (qb, pb))
        return jnp.moveaxis(o, 0, 1).reshape(B, L, *o.shape[3:])
    return diff_attn_rows(q, qpos, k, v, kpos, lam)


def mem_attend(q, mk, mv):
    s = jnp.einsum('bqhd,bmhd->bhqm', q, mk).astype(jnp.float32) * DH_C ** -0.5
    p = jax.nn.softmax(s, axis=-1)
    return jnp.einsum('bhqm,bmhd->bqhd', p.astype(mv.dtype), mv)


def hybrid_layer(x, pos, kpos, k_past, v_past, mem_k, mem_v, conv_buf, s0,
                 g_pre, w_in, conv_w, a_log, dt_bias, g_gdn, lam_vec, g_sub, w_out, lam_init):
    B, L, _ = x.shape
    h = rmsnorm(x, g_pre)
    proj = h @ w_in
    qkv_a, z_a, b_a, a_a, q_b, k_b, v_b, z_b, q_c, z_c = jnp.split(proj, split_points(), -1)
    qkv, conv_new = causal_conv(qkv_a, conv_buf, conv_w)
    qa, ka, va = jnp.split(qkv, [HA * DK_A, 2 * HA * DK_A], -1)
    qa = l2norm(qa.reshape(B, L, HA, DK_A)) * DK_A ** -0.5
    ka = l2norm(ka.reshape(B, L, HA, DK_A))
    va = va.reshape(B, L, HA, DV_A).astype(jnp.float32)
    beta = jax.nn.sigmoid(b_a.astype(jnp.float32))
    g = -jnp.exp(a_log.astype(jnp.float32)) * jax.nn.softplus(
        a_a.astype(jnp.float32) + dt_bias.astype(jnp.float32))
    o_a, s_new = gated_delta_blocks(qa, ka, va, g, beta, s0)
    o_a = rmsnorm(o_a, g_gdn) * jax.nn.silu(z_a.reshape(B, L, HA, DV_A).astype(jnp.float32))
    o_a = o_a.reshape(B, L, WA).astype(x.dtype)
    qb = q_b.reshape(B, L, HB, 2, DQK_B)
    k_new = k_b.reshape(B, L, HB, 2, DQK_B)
    v_new = v_b.reshape(B, L, HB, DV_B)
    if k_past is None:
        k_all, v_all = k_new, v_new
    else:
        k_all = jnp.concatenate([k_past.astype(k_new.dtype), k_new], axis=1)
        v_all = jnp.concatenate([v_past.astype(v_new.dtype), v_new], axis=1)
    lv = lam_vec.astype(jnp.float32)
    lam = jnp.exp(jnp.sum(lv[0] * lv[1])) - jnp.exp(jnp.sum(lv[2] * lv[3])) + lam_init
    o_b = diff_attend(qb, pos, k_all, v_all, kpos, lam)
    o_b = rmsnorm(o_b, g_sub) * (1.0 - lam_init) * jax.nn.silu(z_b.reshape(B, L, HB, DV_B))
    o_c = mem_attend(q_c.reshape(B, L, HC, DH_C), mem_k.astype(x.dtype), mem_v.astype(x.dtype))
    o_c = o_c * jax.nn.silu(z_c.reshape(B, L, HC, DH_C))
    mixed = jnp.concatenate([o_a, o_b.reshape(B, L, WB).astype(x.dtype),
                             o_c.reshape(B, L, WC).astype(x.dtype)], -1)
    y = x + mixed @ w_out
    return y, k_new, v_new, conv_new, s_new.astype(s0.dtype)


def setup_inputs(seed: int = 0) -> dict:
    key = jax.random.key(seed)
    ks = jax.random.split(key, 24)
    f = jnp.float32

    def nrm(k, shape, s):
        return jax.random.normal(k, shape, f) * s

    dt = jnp.exp(jax.random.uniform(ks[13], (DEPTH, HA), f, math.log(1e-3), math.log(1e-1)))
    return {
        'x_prompt': nrm(ks[0], (BATCH, SEQ, D_MODEL), 1.0),
        'x_sample': nrm(ks[1], (DEC_BATCH, DEC_SEQ, D_MODEL), 1.0),
        'mem_prompt': nrm(ks[2], (BATCH, N_MEM, D_MODEL), 1.0),
        'cache_diff_k': nrm(ks[3], (DEPTH, DEC_BATCH, PAST_LEN, HB, 2, DQK_B), 1.0),
        'cache_diff_v': nrm(ks[4], (DEPTH, DEC_BATCH, PAST_LEN, HB, DV_B), 1.0),
        'cache_mem_k': nrm(ks[5], (DEPTH, DEC_BATCH, N_MEM, HC, DH_C), 1.0),
        'cache_mem_v': nrm(ks[6], (DEPTH, DEC_BATCH, N_MEM, HC, DH_C), 1.0),
        'state_gdn': nrm(ks[7], (DEPTH, DEC_BATCH, HA, DK_A, DV_A), 0.1),
        'cache_conv': nrm(ks[8], (DEPTH, DEC_BATCH, CONV_W - 1, QKV_A), 1.0),
        'g_pre': 1.0 + nrm(ks[9], (DEPTH, D_MODEL), 0.02),
        'w_in': nrm(ks[10], (DEPTH, D_MODEL, N_IN), D_MODEL ** -0.5),
        'conv_w': nrm(ks[11], (DEPTH, CONV_W, QKV_A), CONV_W ** -0.5),
        'a_log': jnp.log(jax.random.uniform(ks[12], (DEPTH, HA), f, 1.0, 16.0)),
        'dt_bias': dt + jnp.log(-jnp.expm1(-dt)),
        'g_gdn': 1.0 + nrm(ks[14], (DEPTH, DV_A), 0.02),
        'lam_vec': nrm(ks[15], (DEPTH, 4, DQK_B), 0.1),
        'g_sub': 1.0 + nrm(ks[16], (DEPTH, DV_B), 0.02),
        'g_mem': 1.0 + nrm(ks[17], (DEPTH, D_MODEL), 0.02),
        'w_mem_kv': nrm(ks[18], (DEPTH, D_MODEL, 2 * WC), D_MODEL ** -0.5),
        'w_out': nrm(ks[19], (DEPTH, D_MIX, D_MODEL), D_MIX ** -0.5),
        'g_final': 1.0 + nrm(ks[20], (D_MODEL,), 0.02),
    }


def reference(x_prompt, x_sample, mem_prompt, cache_diff_k, cache_diff_v, cache_mem_k,
              cache_mem_v, state_gdn, cache_conv, g_pre, w_in, conv_w, a_log, dt_bias, g_gdn,
              lam_vec, g_sub, g_mem, w_mem_kv, w_out, g_final):
    B, L = x_prompt.shape[:2]
    Ls = x_sample.shape[1]
    past = cache_diff_k.shape[2]
    pos_p = jnp.arange(L, dtype=jnp.int32)
    pos_s = past + jnp.arange(Ls, dtype=jnp.int32)
    kpos_s = jnp.arange(past + Ls, dtype=jnp.int32)
    hp, hs = x_prompt, x_sample
    kp_l, vp_l, mk_l, mv_l, sp_l, cp_l = [], [], [], [], [], []
    ks_l, vs_l, ss_l, cs_l = [], [], [], []
    for i in range(DEPTH):
        lam_init = 0.8 - 0.6 * math.exp(-0.3 * i)
        wts = (g_pre[i], w_in[i], conv_w[i], a_log[i], dt_bias[i], g_gdn[i], lam_vec[i],
               g_sub[i], w_out[i], lam_init)
        mem_n = rmsnorm(mem_prompt, g_mem[i])
        mk, mv = jnp.split(mem_n @ w_mem_kv[i], 2, -1)
        mk = mk.reshape(B, N_MEM, HC, DH_C)
        mv = mv.reshape(B, N_MEM, HC, DH_C)
        conv0 = jnp.zeros((B, CONV_W - 1, QKV_A), x_prompt.dtype)
        s0 = jnp.zeros((B, HA, DK_A, DV_A), x_prompt.dtype)
        hp, kp, vp, cp, sp = hybrid_layer(hp, pos_p, pos_p, None, None, mk, mv, conv0, s0, *wts)
        hs, kn, vn, cn, sn = hybrid_layer(hs, pos_s, kpos_s, cache_diff_k[i], cache_diff_v[i],
                                          cache_mem_k[i], cache_mem_v[i], cache_conv[i],
                                          state_gdn[i], *wts)
        kp_l.append(kp); vp_l.append(vp); mk_l.append(mk); mv_l.append(mv)
        sp_l.append(sp); cp_l.append(cp)
        ks_l.append(kn); vs_l.append(vn); ss_l.append(sn); cs_l.append(cn)
    y_prompt = rmsnorm(hp, g_final)
    y_sample = rmsnorm(hs, g_final)
    return (y_prompt, y_sample, jnp.stack(kp_l), jnp.stack(vp_l), jnp.stack(mk_l),
            jnp.stack(mv_l), jnp.stack(sp_l), jnp.stack(cp_l), jnp.stack(ks_l),
            jnp.stack(vs_l), jnp.stack(ss_l), jnp.stack(cs_l))
```

```python
import functools
import math

import jax
import jax.numpy as jnp
from jax import lax
from jax.experimental import pallas as pl
from jax.experimental.pallas import tpu as pltpu

F32 = jnp.float32
BF16 = jnp.bfloat16
EPS = 1e-6
CHUNK = 64
LANES = 128
VMEM_LIMIT = 56 * 1024 * 1024
HIGHEST = lax.Precision.HIGHEST


def _pick(n, pref, align):
    if n <= pref:
        return n
    t = (pref // align) * align
    while t > align and n % t:
        t -= align
    assert n % t == 0, (n, pref, align)
    return t


def _sigmoid(x):
    return 1.0 / (1.0 + jnp.exp(-x))


def _silu(x):
    return x * _sigmoid(x)


def _softplus(x):
    return jnp.maximum(x, 0.0) + jnp.log1p(jnp.exp(-jnp.abs(x)))


def _rms(x, g):
    return x * lax.rsqrt(jnp.mean(x * x, axis=-1, keepdims=True) + EPS) * g


def _dot(a, b):
    return jnp.dot(a, b, preferred_element_type=F32)


def _dot_nt(a, b):
    return lax.dot_general(a, b, (((1,), (1,)), ((), ())), preferred_element_type=F32)


def _dot_tn(a, b):
    return lax.dot_general(a, b, (((0,), (0,)), ((), ())), preferred_element_type=F32)


def _params(sem):
    return pltpu.CompilerParams(dimension_semantics=sem, vmem_limit_bytes=VMEM_LIMIT)


COL_CHUNK = 512


def _norm_matmul_kernel(x_ref, g_ref, w_ref, *out_refs, groups):
    hb = _rms(x_ref[...], g_ref[...]).astype(BF16)
    for w_off, width, outs in groups:
        for c0 in range(0, width, COL_CHUNK):
            cw = min(COL_CHUNK, width - c0)
            r = _dot(hb, w_ref[:, w_off + c0:w_off + c0 + cw])
            for oi in outs:
                out_refs[oi][:, c0:c0 + cw] = r.astype(out_refs[oi].dtype)


def _norm_matmul(x2d, g, w_bf16, groups, out_defs, tm_pref=256):
    T, D = x2d.shape
    tm = _pick(T, tm_pref, 16)
    return pl.pallas_call(
        functools.partial(_norm_matmul_kernel, groups=groups),
        grid=(T // tm,),
        in_specs=[
            pl.BlockSpec((tm, D), lambda i: (i, 0)),
            pl.BlockSpec((1, D), lambda i: (0, 0)),
            pl.BlockSpec(w_bf16.shape, lambda i: (0, 0), pipeline_mode=pl.Buffered(1)),
        ],
        out_specs=[pl.BlockSpec((tm, w), lambda i: (i, 0)) for w, _ in out_defs],
        out_shape=[jax.ShapeDtypeStruct((T, w), dt) for w, dt in out_defs],
        compiler_params=_params(("parallel",)),
        name="norm_matmul",
    )(x2d, g.reshape(1, D), w_bf16)


CONV_PAD = 8


def _gdn_kernel(q_ref, k_ref, v_ref, cwq_ref, cwk_ref, cwv_ref, ciq_ref, cik_ref, civ_ref,
                ba_ref, z_ref, alog_ref, dtb_ref, ggdn_ref, s0_ref,
                o_ref, sout_ref,
                xq, xk, xv, qs, ks, vs, beta_s, g_s, o_s, S_s,
                *, C, ST, TT, conv_w, n_heads):
    h = pl.program_id(1)
    t = pl.program_id(2)
    nt = pl.num_programs(2)
    DK = q_ref.shape[-1]
    lead = CONV_PAD - (conv_w - 1)

    @pl.when(t == 0)
    def _():
        xq[0:CONV_PAD, :] = ciq_ref[...]
        xk[0:CONV_PAD, :] = cik_ref[...]
        xv[0:CONV_PAD, :] = civ_ref[...]
        S_s[...] = s0_ref[...]

    @pl.when(t > 0)
    def _():
        xq[0:CONV_PAD, :] = xq[TT:TT + CONV_PAD, :]
        xk[0:CONV_PAD, :] = xk[TT:TT + CONV_PAD, :]
        xv[0:CONV_PAD, :] = xv[TT:TT + CONV_PAD, :]

    xq[CONV_PAD:, :] = q_ref[...]
    xk[CONV_PAD:, :] = k_ref[...]
    xv[CONV_PAD:, :] = v_ref[...]

    def conv_silu(xbuf, cw_ref):
        w = cw_ref[...]
        y = xbuf[lead:lead + TT, :] * w[0:1, :]
        for i in range(1, conv_w):
            y = y + xbuf[lead + i:lead + i + TT, :] * w[i:i + 1, :]
        return _silu(y)

    def l2n(y):
        return y * lax.rsqrt(jnp.sum(y * y, axis=-1, keepdims=True) + EPS)

    qs[...] = l2n(conv_silu(xq, cwq_ref)) * (DK ** -0.5)
    ks[...] = l2n(conv_silu(xk, cwk_ref))
    vs[...] = conv_silu(xv, cwv_ref)

    rowi = lax.broadcasted_iota(jnp.int32, (LANES, 2 * LANES), 0)
    coli = lax.broadcasted_iota(jnp.int32, (LANES, 2 * LANES), 1)
    sel = jnp.where(rowi == jnp.where(coli < LANES, h, h + n_heads), 1.0, 0.0)
    bacol = jnp.dot(ba_ref[...], sel, precision=HIGHEST, preferred_element_type=F32)
    beta_s[...] = _sigmoid(bacol[:, :LANES])
    neg_a = -jnp.exp(alog_ref[pl.ds(h, 1), :])
    g_s[...] = neg_a * _softplus(bacol[:, LANES:] + dtb_ref[pl.ds(h, 1), :])

    ri = lax.broadcasted_iota(jnp.int32, (ST, ST), 0)
    ci = lax.broadcasted_iota(jnp.int32, (ST, ST), 1)
    same = (ri // C) == (ci // C)
    mask_c = same & (ci <= ri)
    mask_s = same & (ci < ri)
    tri = jnp.where(mask_c, 1.0, 0.0)
    n_sq = int(math.log2(C)) - 1

    def subtile(s, carry):
        r0 = pl.multiple_of(s * ST, ST)
        q = qs[pl.ds(r0, ST), :]
        k = ks[pl.ds(r0, ST), :]
        v = vs[pl.ds(r0, ST), :]
        beta = beta_s[pl.ds(r0, ST), :]
        g = g_s[pl.ds(r0, ST), :]
        Gc = jnp.dot(tri, g, precision=HIGHEST, preferred_element_type=F32)
        eG = jnp.exp(Gc)
        Gi = Gc[:, :ST]
        Gj = Gc.T[:ST, :]
        decay = jnp.exp(jnp.where(mask_c, Gi - Gj, -jnp.inf))
        kb = k.astype(BF16)
        qk_kk = _dot_nt(jnp.concatenate([q.astype(BF16), kb], axis=0), kb)
        attn = qk_kk[:ST] * decay
        lmat = jnp.where(mask_s, beta[:, :ST] * qk_kk[ST:] * decay, 0.0)
        Qm = -lmat
        Mb = lmat.astype(BF16)
        for _ in range(n_sq):
            M = _dot(Mb, Mb)
            Mb = M.astype(BF16)
            Qm = Qm + M + _dot(Qm.astype(BF16), Mb)
        rhs = jnp.concatenate([v * beta, k * (beta * eG)], axis=-1)
        sol = rhs + _dot(Qm.astype(BF16), rhs.astype(BF16))
        U = sol[:, :DK]
        W = sol[:, DK:]
        q_dec = q * eG
        for c in range(ST // C):
            lo, hi = c * C, (c + 1) * C
            g_last = Gc[hi - 1:hi, :]
            k_tail = k[lo:hi] * jnp.exp(g_last - Gc[lo:hi])
            S = S_s[...]
            Sb = S.astype(BF16)
            r = _dot(jnp.concatenate([W[lo:hi], q_dec[lo:hi]], axis=0).astype(BF16), Sb)
            e = U[lo:hi] - r[:C]
            eb = e.astype(BF16)
            parts = [eb if j == c else jnp.zeros_like(eb) for j in range(ST // C)]
            e_pad = parts[0] if len(parts) == 1 else jnp.concatenate(parts, axis=0)
            o_c = r[C:] + _dot(attn[lo:hi, :].astype(BF16), e_pad)
            S_s[...] = S * jnp.exp(g_last) + _dot_tn(k_tail.astype(BF16), eb)
            o_s[pl.ds(r0 + lo, C), :] = o_c
        return carry

    lax.fori_loop(0, TT // ST, subtile, 0)

    o_ref[...] = (_rms(o_s[...], ggdn_ref[...]) * _silu(z_ref[...])).astype(o_ref.dtype)

    @pl.when(t == nt - 1)
    def _():
        sout_ref[...] = S_s[...]


def _gdn(qkv, z_a, ba, conv_w, conv_init, s0, a_log, dt_bias, g_gdn, B, L, tt_pref=256):
    H = a_log.shape[0]
    DK = qkv.shape[1] // (3 * H)
    assert DK == LANES and s0.shape == (B, H, DK, DK)
    CW = conv_w.shape[0]
    C = min(CHUNK, L)
    TT = _pick(L, tt_pref, C)
    ST = min(TT, 2 * C)
    assert L % C == 0 and TT % ST == 0 and ST % C == 0 and C & (C - 1) == 0
    NT = L // TT
    cinit = jnp.concatenate(
        [jnp.zeros((B, CONV_PAD - (CW - 1), qkv.shape[1]), F32), conv_init.astype(F32)], axis=1)
    alog_b = jnp.broadcast_to(a_log.astype(F32)[:, None], (H, LANES))
    dtb_b = jnp.broadcast_to(dt_bias.astype(F32)[:, None], (H, LANES))

    def row(b, h, t):
        return b * NT + t

    tile = lambda off: pl.BlockSpec((TT, DK), lambda b, h, t: (row(b, h, t), off * H + h))
    cws = lambda off: pl.BlockSpec((CW, DK), lambda b, h, t: (0, off * H + h))
    cis = lambda off: pl.BlockSpec((None, CONV_PAD, DK), lambda b, h, t: (b, 0, off * H + h))
    full = lambda shp: pl.BlockSpec(shp, lambda b, h, t: (0, 0))
    st_spec = pl.BlockSpec((None, None, DK, DK), lambda b, h, t: (b, h, 0, 0))
    return pl.pallas_call(
        functools.partial(_gdn_kernel, C=C, ST=ST, TT=TT, conv_w=CW, n_heads=H),
        grid=(B, H, NT),
        in_specs=[tile(0), tile(1), tile(2), cws(0), cws(1), cws(2), cis(0), cis(1), cis(2),
                  pl.BlockSpec((TT, LANES), lambda b, h, t: (row(b, h, t), 0)),
                  pl.BlockSpec((TT, DK), lambda b, h, t: (row(b, h, t), h)),
                  full((H, LANES)), full((H, LANES)), full((1, DK)), st_spec],
        out_specs=[pl.BlockSpec((TT, DK), lambda b, h, t: (row(b, h, t), h)), st_spec],
        out_shape=[jax.ShapeDtypeStruct((B * L, H * DK), BF16),
                   jax.ShapeDtypeStruct((B, H, DK, DK), F32)],
        scratch_shapes=[pltpu.VMEM((TT + CONV_PAD, DK), F32)] * 3
        + [pltpu.VMEM((TT, DK), F32)] * 6 + [pltpu.VMEM((DK, DK), F32)],
        compiler_params=_params(("parallel", "parallel", "arbitrary")),
        name="gdn",
    )(qkv, qkv, qkv, conv_w, conv_w, conv_w, cinit, cinit, cinit, ba, z_a,
      alog_b, dtb_b, g_gdn.reshape(1, DK).astype(F32), s0.astype(F32))


def _stack_maps(q, dqk):
    lane = lax.broadcasted_iota(jnp.int32, q.shape, 1)
    qf = q.astype(F32) * (dqk ** -0.5)
    stacked = jnp.concatenate([jnp.where(lane < dqk, qf, 0.0), jnp.where(lane >= dqk, qf, 0.0)], axis=0)
    return stacked.astype(BF16)


def _diff_finish(acc, l, lv_ref, gsub_ref, z_ref, o_ref, tq, lam_init):
    lv = lv_ref[...]
    lam = (jnp.exp(jnp.sum(lv[0:1] * lv[1:2], axis=-1, keepdims=True))
           - jnp.exp(jnp.sum(lv[2:3] * lv[3:4], axis=-1, keepdims=True)) + lam_init)
    o = acc[:tq] / l[:tq] - lam * (acc[tq:] / l[tq:])
    o_ref[...] = (_rms(o, gsub_ref[...]) * (1.0 - lam_init) * _silu(z_ref[...])).astype(o_ref.dtype)


def _diff_attn_kernel(q_ref, k_ref, v_ref, z_ref, lv_ref, gsub_ref, o_ref, *, tq, dqk, lam_init):
    qi = pl.program_id(2)
    dv = v_ref.shape[-1]
    qq = _stack_maps(q_ref[...], dqk)
    ri = lax.broadcasted_iota(jnp.int32, (2 * tq, tq), 0)
    ci = lax.broadcasted_iota(jnp.int32, (2 * tq, tq), 1)
    ri = jnp.where(ri >= tq, ri - tq, ri)
    visible = (ci // CHUNK) <= (ri // CHUNK)

    def tile(ki, carry, masked):
        m, l, acc = carry
        k0 = pl.multiple_of(ki * tq, tq)
        s = _dot_nt(qq, k_ref[pl.ds(k0, tq), :].astype(BF16))
        if masked:
            s = jnp.where(visible, s, -jnp.inf)
        m_new = jnp.maximum(m, jnp.max(s, axis=-1, keepdims=True))
        alpha = jnp.exp(m - m_new)
        p = jnp.exp(s - m_new)
        l = alpha * l + jnp.sum(p, axis=-1, keepdims=True)
        acc = alpha * acc + _dot(p.astype(BF16), v_ref[pl.ds(k0, tq), :].astype(BF16))
        return m_new, l, acc

    init = (jnp.full((2 * tq, 1), -jnp.inf, F32), jnp.zeros((2 * tq, 1), F32),
            jnp.zeros((2 * tq, dv), F32))
    carry = lax.fori_loop(0, qi, functools.partial(tile, masked=False), init)
    _, l, acc = tile(qi, carry, True)
    _diff_finish(acc, l, lv_ref, gsub_ref, z_ref, o_ref, tq, lam_init)


def _diff_attn(q, k, v, z, lam_vec, g_sub, B, L, lam_init, tq_pref=256):
    dqk = lam_vec.shape[-1]
    HB = q.shape[1] // (2 * dqk)
    dv = v.shape[1] // HB
    assert 2 * dqk == LANES and dv == LANES
    tq = _pick(L, tq_pref, CHUNK)
    assert L % tq == 0 and tq % CHUNK == 0
    NQ = L // tq
    qspec = pl.BlockSpec((tq, LANES), lambda b, h, i: (b * NQ + i, h))
    kspec = pl.BlockSpec((L, LANES), lambda b, h, i: (b, h))
    return pl.pallas_call(
        functools.partial(_diff_attn_kernel, tq=tq, dqk=dqk, lam_init=lam_init),
        grid=(B, HB, NQ),
        in_specs=[qspec, kspec, kspec, qspec,
                  pl.BlockSpec(lam_vec.shape, lambda b, h, i: (0, 0)),
                  pl.BlockSpec((1, dv), lambda b, h, i: (0, 0))],
        out_specs=qspec,
        out_shape=jax.ShapeDtypeStruct((B * L, HB * dv), BF16),
        compiler_params=_params(("parallel", "parallel", "arbitrary")),
        name="diff_attn",
    )(q, k, v, z, lam_vec.astype(F32), g_sub.reshape(1, dv).astype(F32))


def _diff_attn_kv_kernel(q_ref, kp_ref, vp_ref, kn_ref, vn_ref, z_ref, lv_ref, gsub_ref, o_ref,
                         *, dqk, lam_init):
    tq = q_ref.shape[0]
    P = kp_ref.shape[0]
    qq = _stack_maps(q_ref[...], dqk)

    def scores(k, kpos0):
        n = k.shape[0]
        ri = lax.broadcasted_iota(jnp.int32, (2 * tq, n), 0)
        ci = lax.broadcasted_iota(jnp.int32, (2 * tq, n), 1)
        qpos = P + jnp.where(ri >= tq, ri - tq, ri)
        visible = ((kpos0 + ci) // CHUNK) <= (qpos // CHUNK)
        return jnp.where(visible, _dot_nt(qq, k.astype(BF16)), -jnp.inf)

    s_p = scores(kp_ref[...], 0)
    s_n = scores(kn_ref[...], P)
    m = jnp.maximum(jnp.max(s_p, axis=-1, keepdims=True), jnp.max(s_n, axis=-1, keepdims=True))
    p_p = jnp.exp(s_p - m)
    p_n = jnp.exp(s_n - m)
    l = jnp.sum(p_p, axis=-1, keepdims=True) + jnp.sum(p_n, axis=-1, keepdims=True)
    acc = _dot(p_p.astype(BF16), vp_ref[...].astype(BF16)) + _dot(p_n.astype(BF16), vn_ref[...].astype(BF16))
    _diff_finish(acc, l, lv_ref, gsub_ref, z_ref, o_ref, tq, lam_init)


def _diff_attn_kv(q, k_past, v_past, k_new, v_new, z, lam_vec, g_sub, B, L, lam_init):
    dqk = lam_vec.shape[-1]
    HB = q.shape[1] // (2 * dqk)
    dv = v_new.shape[1] // HB
    P = k_past.shape[1]
    assert 2 * dqk == LANES and dv == LANES
    qspec = pl.BlockSpec((L, LANES), lambda b, h: (b, h))
    pspec = pl.BlockSpec((None, P, LANES), lambda b, h: (b, 0, h))
    return pl.pallas_call(
        functools.partial(_diff_attn_kv_kernel, dqk=dqk, lam_init=lam_init),
        grid=(B, HB),
        in_specs=[qspec, pspec, pspec, qspec, qspec, qspec,
                  pl.BlockSpec(lam_vec.shape, lambda b, h: (0, 0)),
                  pl.BlockSpec((1, dv), lambda b, h: (0, 0))],
        out_specs=qspec,
        out_shape=jax.ShapeDtypeStruct((B * L, HB * dv), BF16),
        compiler_params=_params(("parallel", "parallel")),
        name="diff_attn_kv",
    )(q, k_past, v_past, k_new, v_new, z, lam_vec.astype(F32), g_sub.reshape(1, dv).astype(F32))


def _mem_attn_kernel(q_ref, mk_ref, mv_ref, z_ref, o_ref, *, scale):
    s = _dot_nt(q_ref[...], mk_ref[...].astype(BF16)) * scale
    p = jnp.exp(s - jnp.max(s, axis=-1, keepdims=True))
    o = _dot(p.astype(BF16), mv_ref[...].astype(BF16)) / jnp.sum(p, axis=-1, keepdims=True)
    o_ref[...] = (o * _silu(z_ref[...])).astype(o_ref.dtype)


def _mem_attn(q, mk, mv, z, B, L, n_mem, tq_pref=1024):
    dh = LANES
    HC = q.shape[1] // dh
    tq = _pick(L, tq_pref, 16)
    NQ = L // tq
    qspec = pl.BlockSpec((tq, dh), lambda b, h, i: (b * NQ + i, h))
    mspec = pl.BlockSpec((n_mem, dh), lambda b, h, i: (b, h))
    return pl.pallas_call(
        functools.partial(_mem_attn_kernel, scale=dh ** -0.5),
        grid=(B, HC, NQ),
        in_specs=[qspec, mspec, mspec, qspec],
        out_specs=qspec,
        out_shape=jax.ShapeDtypeStruct((B * L, HC * dh), BF16),
        compiler_params=_params(("parallel", "parallel", "parallel")),
        name="mem_attn",
    )(q, mk, mv, z)


def _out_proj_kernel(oa_ref, ob_ref, oc_ref, x_ref, w_ref, g_ref, y_ref, *, final_norm):
    wa, wb = oa_ref.shape[1], ob_ref.shape[1]
    y = (x_ref[...] + _dot(oa_ref[...], w_ref[0:wa, :]) + _dot(ob_ref[...], w_ref[wa:wa + wb, :])
         + _dot(oc_ref[...], w_ref[wa + wb:, :]))
    y_ref[...] = _rms(y, g_ref[...]) if final_norm else y


def _out_proj(o_a, o_b, o_c, x2d, w_bf16, g_final, final_norm, tm_pref=512):
    T, D = x2d.shape
    tm = _pick(T, tm_pref, 16)
    rows = lambda w: pl.BlockSpec((tm, w), lambda i: (i, 0))
    return pl.pallas_call(
        functools.partial(_out_proj_kernel, final_norm=final_norm),
        grid=(T // tm,),
        in_specs=[rows(o_a.shape[1]), rows(o_b.shape[1]), rows(o_c.shape[1]), rows(D),
                  pl.BlockSpec(w_bf16.shape, lambda i: (0, 0)),
                  pl.BlockSpec((1, D), lambda i: (0, 0))],
        out_specs=rows(D),
        out_shape=jax.ShapeDtypeStruct((T, D), F32),
        compiler_params=_params(("parallel",)),
        name="out_proj",
    )(o_a, o_b, o_c, x2d, w_bf16, g_final.reshape(1, D).astype(F32))


def _split_w_in(w_in, sizes):
    offs = [0]
    for s in sizes:
        offs.append(offs[-1] + s)
    return [w_in[:, offs[i]:offs[i + 1]] for i in range(len(sizes))]


def _layer(x, k_past, v_past, mem_k, mem_v, conv_init, s0, w_in_perm, conv_w, a_log, dt_bias, g_pre,
           g_gdn, lam_vec, g_sub, w_out_bf16, g_final, lam_init, final_norm, dims):
    B, L, D = x.shape
    QKV_A, WA, WB, WC, n_mem = dims
    x2d = x.reshape(B * L, D)
    widths = (QKV_A, WA, WB, WB, WB, WC, WB, WC, LANES)
    offs = [0]
    for w in widths:
        offs.append(offs[-1] + w)
    groups = ((offs[0], QKV_A, (0,)), (offs[1], WA, (1,)), (offs[2], WB, (2, 3)), (offs[3], WB, (4, 5)),
              (offs[4], WB, (6,)), (offs[5], WC, (7,)), (offs[6], WB, (8,)), (offs[7], WC, (9,)),
              (offs[8], LANES, (10,)))
    out_defs = ((QKV_A, F32), (WA, F32), (WB, F32), (WB, BF16), (WB, F32), (WB, BF16), (WB, F32),
                (WC, F32), (WB, BF16), (WC, BF16), (LANES, F32))
    (qkv_a, z_a, k_b, k_b16, v_b, v_b16, z_b, z_c, q_b16, q_c16, ba) = _norm_matmul(
        x2d, g_pre, w_in_perm, groups, out_defs)

    o_a, s_new = _gdn(qkv_a, z_a, ba, conv_w, conv_init, s0, a_log, dt_bias, g_gdn, B, L)
    if k_past is None:
        o_b = _diff_attn(q_b16, k_b16, v_b16, z_b, lam_vec, g_sub, B, L, lam_init)
    else:
        o_b = _diff_attn_kv(q_b16, k_past, v_past, k_b16, v_b16, z_b, lam_vec, g_sub, B, L, lam_init)
    o_c = _mem_attn(q_c16, mem_k, mem_v, z_c, B, L, n_mem)
    y = _out_proj(o_a, o_b, o_c, x2d, w_out_bf16, g_final, final_norm).reshape(B, L, D)
    conv_new = qkv_a.reshape(B, L, QKV_A)[:, L - (conv_w.shape[0] - 1):, :]
    return y, k_b, v_b, conv_new, s_new


def kernel(x_prompt, x_sample, mem_prompt, cache_diff_k, cache_diff_v, cache_mem_k, cache_mem_v,
           state_gdn, cache_conv, g_pre, w_in, conv_w, a_log, dt_bias, g_gdn, lam_vec, g_sub, g_mem,
           w_mem_kv, w_out, g_final):
    depth = w_in.shape[0]
    B, L, D = x_prompt.shape
    Bs, Ls, _ = x_sample.shape
    _, _, P, HB, _, DQK = cache_diff_k.shape
    DV_B = cache_diff_v.shape[-1]
    _, _, n_mem, HC, DH_C = cache_mem_k.shape
    _, _, HA, DK_A, DV_A = state_gdn.shape
    CW = conv_w.shape[1]
    QKV_A = conv_w.shape[2]
    WA, WB, WC = HA * DV_A, HB * DV_B, HC * DH_C
    assert QKV_A == HA * (2 * DK_A + DV_A) and DK_A == DV_A == LANES and DH_C == LANES
    in_sizes = (QKV_A, WA, HA, HA, HB * 2 * DQK, HB * 2 * DQK, WB, WB, WC, WC)
    assert w_in.shape[2] == sum(in_sizes) and 2 * HA <= LANES
    dims = (QKV_A, WA, WB, WC, n_mem)

    hp, hs = x_prompt, x_sample
    outs = [[] for _ in range(10)]
    for i in range(depth):
        lam_init = 0.8 - 0.6 * math.exp(-0.3 * i)
        final_norm = i == depth - 1
        qkv_w, za_w, b_w, a_w, qb_w, kb_w, vb_w, zb_w, qc_w, zc_w = _split_w_in(w_in[i], in_sizes)
        ba_w = jnp.concatenate([b_w, a_w, jnp.zeros((D, LANES - 2 * HA), w_in.dtype)], axis=1)
        w_in_perm = jnp.concatenate(
            [qkv_w, za_w, kb_w, vb_w, zb_w, zc_w, qb_w, qc_w, ba_w], axis=1).astype(BF16)
        w_out_bf16 = w_out[i].astype(BF16)

        mk, mk16, mv, mv16 = _norm_matmul(
            mem_prompt.reshape(B * n_mem, D), g_mem[i], w_mem_kv[i].astype(BF16),
            ((0, WC, (0, 1)), (WC, WC, (2, 3))), ((WC, F32), (WC, BF16), (WC, F32), (WC, BF16)))

        wts = (w_in_perm, conv_w[i], a_log[i], dt_bias[i], g_pre[i], g_gdn[i], lam_vec[i], g_sub[i],
               w_out_bf16, g_final, lam_init, final_norm, dims)
        hp, kp, vp, cp, sp = _layer(
            hp, None, None, mk16, mv16, jnp.zeros((B, CW - 1, QKV_A), F32),
            jnp.zeros((B, HA, DK_A, DV_A), F32), *wts)
        hs, kn, vn, cn, sn = _layer(
            hs, cache_diff_k[i].reshape(Bs, P, HB * 2 * DQK), cache_diff_v[i].reshape(Bs, P, WB),
            cache_mem_k[i].reshape(Bs * n_mem, WC), cache_mem_v[i].reshape(Bs * n_mem, WC),
            cache_conv[i], state_gdn[i], *wts)
        layer_out = (kp.reshape(B, L, HB, 2, DQK), vp.reshape(B, L, HB, DV_B),
                     mk.reshape(B, n_mem, HC, DH_C), mv.reshape(B, n_mem, HC, DH_C), sp, cp,
                     kn.reshape(Bs, Ls, HB, 2, DQK), vn.reshape(Bs, Ls, HB, DV_B), sn, cn)
        for lst, o in zip(outs, layer_out):
            lst.append(o)
    return (hp, hs) + tuple(jnp.stack(o) for o in outs)
```

```python
import functools
import math

import jax
import jax.numpy as jnp
from jax import lax
from jax.experimental import pallas as pl
from jax.experimental.pallas import tpu as pltpu

F32 = jnp.float32
BF16 = jnp.bfloat16
EPS = 1e-6
CHUNK = 64
LANES = 128
VMEM_LIMIT = 56 * 1024 * 1024
HIGHEST = lax.Precision.HIGHEST


def _pick(n, pref, align):
    if n <= pref:
        return n
    t = (pref // align) * align
    while t > align and n % t:
        t -= align
    assert n % t == 0, (n, pref, align)
    return t


def _sigmoid(x):
    return 1.0 / (1.0 + jnp.exp(-x))


def _silu(x):
    return x * _sigmoid(x)


def _softplus(x):
    return jnp.maximum(x, 0.0) + jnp.log(1.0 + jnp.exp(-jnp.abs(x)))


def _rms(x, g):
    return x * lax.rsqrt(jnp.mean(x * x, axis=-1, keepdims=True) + EPS) * g


def _dot(a, b):
    return jnp.dot(a, b, preferred_element_type=F32)


def _dot_nt(a, b):
    return lax.dot_general(a, b, (((1,), (1,)), ((), ())), preferred_element_type=F32)


def _dot_tn(a, b):
    return lax.dot_general(a, b, (((0,), (0,)), ((), ())), preferred_element_type=F32)


def _params(sem):
    return pltpu.CompilerParams(dimension_semantics=sem, vmem_limit_bytes=VMEM_LIMIT)


COL_CHUNK = 512


def _norm_matmul_kernel(x_ref, g_ref, w_ref, *out_refs, groups):
    hb = _rms(x_ref[...], g_ref[...]).astype(BF16)
    for w_off, width, outs in groups:
        for c0 in range(0, width, COL_CHUNK):
            cw = min(COL_CHUNK, width - c0)
            r = _dot(hb, w_ref[:, w_off + c0:w_off + c0 + cw])
            for oi in outs:
                out_refs[oi][:, c0:c0 + cw] = r.astype(out_refs[oi].dtype)


def _norm_matmul(x2d, g, w_bf16, groups, out_defs, tm_pref=256):
    T, D = x2d.shape
    tm = _pick(T, tm_pref, 16)
    return pl.pallas_call(
        functools.partial(_norm_matmul_kernel, groups=groups),
        grid=(T // tm,),
        in_specs=[
            pl.BlockSpec((tm, D), lambda i: (i, 0)),
            pl.BlockSpec((1, D), lambda i: (0, 0)),
            pl.BlockSpec(w_bf16.shape, lambda i: (0, 0), pipeline_mode=pl.Buffered(1)),
        ],
        out_specs=[pl.BlockSpec((tm, w), lambda i: (i, 0)) for w, _ in out_defs],
        out_shape=[jax.ShapeDtypeStruct((T, w), dt) for w, dt in out_defs],
        compiler_params=_params(("parallel",)),
        name="norm_matmul",
    )(x2d, g.reshape(1, D), w_bf16)


CONV_PAD = 8
GDN_HEADS = 4


def _split3(x):
    hi = x.astype(BF16)
    r1 = x - hi.astype(F32)
    mid = r1.astype(BF16)
    lo = (r1 - mid.astype(F32)).astype(BF16)
    return hi, mid, lo


def _gdn_kernel(q_ref, k_ref, v_ref, cwq_ref, cwk_ref, cwv_ref, ciq_ref, cik_ref, civ_ref,
                ba_ref, z_ref, alog_ref, dtb_ref, ggdn_ref, s0_ref,
                o_ref, sout_ref,
                xq, xk, xv, S_s,
                *, C, ST, TT, conv_w, n_heads, G):
    hg = pl.program_id(1)
    t = pl.program_id(2)
    nt = pl.num_programs(2)
    DK = LANES
    lead = CONV_PAD - (conv_w - 1)
    n_sub = TT // ST
    n_chunk = ST // C
    heads = [slice(j * DK, (j + 1) * DK) for j in range(G)]

    @pl.when(t == 0)
    def _():
        xq[0:CONV_PAD, :] = ciq_ref[...]
        xk[0:CONV_PAD, :] = cik_ref[...]
        xv[0:CONV_PAD, :] = civ_ref[...]
        S_s[...] = s0_ref[...]

    @pl.when(t > 0)
    def _():
        xq[0:CONV_PAD, :] = xq[TT:TT + CONV_PAD, :]
        xk[0:CONV_PAD, :] = xk[TT:TT + CONV_PAD, :]
        xv[0:CONV_PAD, :] = xv[TT:TT + CONV_PAD, :]

    xq[CONV_PAD:, :] = q_ref[...]
    xk[CONV_PAD:, :] = k_ref[...]
    xv[CONV_PAD:, :] = v_ref[...]

    def conv_silu(xbuf, cw_ref, hs):
        w = cw_ref[:, hs]
        y = xbuf[lead:lead + TT, hs] * w[0:1, :]
        for i in range(1, conv_w):
            y = y + xbuf[lead + i:lead + i + TT, hs] * w[i:i + 1, :]
        return _silu(y)

    def l2n(y):
        return y * lax.rsqrt(jnp.sum(y * y, axis=-1, keepdims=True) + EPS)

    ri = lax.broadcasted_iota(jnp.int32, (ST, ST), 0)
    ci = lax.broadcasted_iota(jnp.int32, (ST, ST), 1)
    same = (ri // C) == (ci // C)
    mask_c = same & (ci <= ri)
    mask_s = same & (ci < ri)
    tri = jnp.where(mask_c, 1.0, 0.0).astype(BF16)
    n_sq = int(math.log2(C)) - 1

    ba = ba_ref[...]
    lane = lax.broadcasted_iota(jnp.int32, ba.shape, 1)
    gate = jnp.where(lane < n_heads, _sigmoid(ba),
                     -jnp.exp(alog_ref[...]) * _softplus(ba + dtb_ref[...]))
    g3 = jnp.concatenate(_split3(gate), axis=-1)
    cums = []
    for s in range(n_sub):
        c3 = _dot(tri, g3[s * ST:(s + 1) * ST])
        cums.append(c3[:, :LANES] + c3[:, LANES:2 * LANES] + c3[:, 2 * LANES:])
    cum = cums[0] if n_sub == 1 else jnp.concatenate(cums, axis=0)
    gc3 = jnp.concatenate(_split3(jnp.concatenate([gate, cum], axis=-1)), axis=0)
    rowi = lax.broadcasted_iota(jnp.int32, (2 * LANES, 2 * LANES), 0)
    coli = lax.broadcasted_iota(jnp.int32, (2 * LANES, 2 * LANES), 1)

    qkvs = []
    for j in range(G):
        hs = heads[j]
        qkvs.append((l2n(conv_silu(xq, cwq_ref, hs)) * (DK ** -0.5), l2n(conv_silu(xk, cwk_ref, hs)),
                     conv_silu(xv, cwv_ref, hs)))

    pairs = [(j, s) for s in range(n_sub) for j in range(G)]
    rows = {p: slice(p[1] * ST, (p[1] + 1) * ST) for p in pairs}
    reps = {}
    for j, s in pairs:
        h = hg * G + j
        lo_r = s * ST
        sel = jnp.where(rowi == jnp.where(coli < LANES, h, LANES + n_heads + h), 1.0, 0.0).astype(BF16)
        reps[j, s] = _dot(jnp.concatenate([gc3[i * TT + lo_r:i * TT + lo_r + ST] for i in range(3)], axis=0), sel)
    qkk = {p: _dot_nt(jnp.concatenate([qkvs[p[0]][0][rows[p]].astype(BF16),
                                       qkvs[p[0]][1][rows[p]].astype(BF16)], axis=0),
                      qkvs[p[0]][1][rows[p]].astype(BF16)) for p in pairs}
    beta, Gc, eG, attn, Qm, Mb = {}, {}, {}, {}, {}, {}
    for p in pairs:
        rep = reps[p][:ST] + reps[p][ST:2 * ST] + reps[p][2 * ST:]
        beta[p] = rep[:, :LANES]
        Gc[p] = rep[:, LANES:]
        eG[p] = jnp.exp(Gc[p])
        decay = jnp.exp(jnp.where(mask_c, Gc[p][:, :ST] - Gc[p].T[:ST, :], -jnp.inf))
        attn[p] = (qkk[p][:ST] * decay).astype(BF16)
        lmat = jnp.where(mask_s, beta[p][:, :ST] * qkk[p][ST:] * decay, 0.0)
        Qm[p] = -lmat
        Mb[p] = lmat.astype(BF16)
    for _ in range(n_sq):
        M = {p: _dot(Mb[p], Mb[p]) for p in pairs}
        Mb = {p: M[p].astype(BF16) for p in pairs}
        QM = {p: _dot(Qm[p].astype(BF16), Mb[p]) for p in pairs}
        Qm = {p: Qm[p] + M[p] + QM[p] for p in pairs}
    rhs = {}
    for p in pairs:
        q, k, v = (a[rows[p]] for a in qkvs[p[0]])
        rhs[p] = jnp.concatenate([v * beta[p], k * (beta[p] * eG[p])], axis=-1)
    corr = {p: _dot(Qm[p].astype(BF16), rhs[p].astype(BF16)) for p in pairs}
    chunk = {}
    for p in pairs:
        q, k, v = (a[rows[p]] for a in qkvs[p[0]])
        sol = rhs[p] + corr[p]
        for c in range(n_chunk):
            lo, hi = c * C, (c + 1) * C
            g_last = Gc[p][hi - 1:hi, :]
            k_tail = (k[lo:hi] * jnp.exp(g_last - Gc[p][lo:hi])).astype(BF16)
            wq = jnp.concatenate([sol[lo:hi, DK:], q[lo:hi] * eG[p][lo:hi]], axis=0).astype(BF16)
            chunk[p, c] = (sol[lo:hi, :DK], wq, attn[p][lo:hi, :], k_tail, jnp.exp(g_last))

    outs = [[] for _ in range(G)]
    for s in range(n_sub):
        for c in range(n_chunk):
            S = [S_s[j] for j in range(G)]
            r = [_dot(chunk[(j, s), c][1], S[j].astype(BF16)) for j in range(G)]
            eb = [(chunk[(j, s), c][0] - r[j][:C]).astype(BF16) for j in range(G)]
            for j in range(G):
                U, wq, attn_c, k_tail, gl = chunk[(j, s), c]
                parts = [eb[j] if i == c else jnp.zeros_like(eb[j]) for i in range(n_chunk)]
                e_pad = parts[0] if n_chunk == 1 else jnp.concatenate(parts, axis=0)
                outs[j].append(r[j][C:] + _dot(attn_c, e_pad))
                S_s[j] = S[j] * gl + _dot_tn(k_tail, eb[j])

    for j in range(G):
        o = outs[j][0] if len(outs[j]) == 1 else jnp.concatenate(outs[j], axis=0)
        o_ref[:, heads[j]] = (_rms(o, ggdn_ref[...]) * _silu(z_ref[:, heads[j]])).astype(o_ref.dtype)

    @pl.when(t == nt - 1)
    def _():
        sout_ref[...] = S_s[...]


def _gdn(qkv, z_a, ba, conv_w, conv_init, s0, a_log, dt_bias, g_gdn, B, L, tt_pref=256):
    H = a_log.shape[0]
    DK = qkv.shape[1] // (3 * H)
    assert DK == LANES and s0.shape == (B, H, DK, DK)
    CW = conv_w.shape[0]
    C = min(CHUNK, L)
    TT = _pick(L, tt_pref, C)
    ST = min(TT, 2 * C)
    G = min(GDN_HEADS, H)
    assert L % C == 0 and TT % ST == 0 and ST % C == 0 and C & (C - 1) == 0 and H % G == 0
    NT = L // TT
    NG = H // G
    W = G * DK
    cinit = jnp.concatenate(
        [jnp.zeros((B, CONV_PAD - (CW - 1), qkv.shape[1]), F32), conv_init.astype(F32)], axis=1)
    pad = jnp.zeros((LANES - 2 * H,), F32)
    alog_v = jnp.concatenate([jnp.zeros((H,), F32), a_log.astype(F32), pad]).reshape(1, LANES)
    dtb_v = jnp.concatenate([jnp.zeros((H,), F32), dt_bias.astype(F32), pad]).reshape(1, LANES)

    def row(b, g, t):
        return b * NT + t

    tile = lambda off: pl.BlockSpec((TT, W), lambda b, g, t: (row(b, g, t), off * NG + g))
    cws = lambda off: pl.BlockSpec((CW, W), lambda b, g, t: (0, off * NG + g))
    cis = lambda off: pl.BlockSpec((None, CONV_PAD, W), lambda b, g, t: (b, 0, off * NG + g))
    full = lambda shp: pl.BlockSpec(shp, lambda b, g, t: (0, 0))
    st_spec = pl.BlockSpec((None, G, DK, DK), lambda b, g, t: (b, g, 0, 0))
    return pl.pallas_call(
        functools.partial(_gdn_kernel, C=C, ST=ST, TT=TT, conv_w=CW, n_heads=H, G=G),
        grid=(B, NG, NT),
        in_specs=[tile(0), tile(1), tile(2), cws(0), cws(1), cws(2), cis(0), cis(1), cis(2),
                  pl.BlockSpec((TT, LANES), lambda b, g, t: (row(b, g, t), 0)),
                  pl.BlockSpec((TT, W), lambda b, g, t: (row(b, g, t), g)),
                  full((1, LANES)), full((1, LANES)), full((1, DK)), st_spec],
        out_specs=[pl.BlockSpec((TT, W), lambda b, g, t: (row(b, g, t), g)), st_spec],
        out_shape=[jax.ShapeDtypeStruct((B * L, H * DK), BF16),
                   jax.ShapeDtypeStruct((B, H, DK, DK), F32)],
        scratch_shapes=[pltpu.VMEM((TT + CONV_PAD, W), F32)] * 3 + [pltpu.VMEM((G, DK, DK), F32)],
        compiler_params=_params(("parallel", "parallel", "arbitrary")),
        name="gdn",
    )(qkv, qkv, qkv, conv_w, conv_w, conv_w, cinit, cinit, cinit, ba, z_a,
      alog_v, dtb_v, g_gdn.reshape(1, DK).astype(F32), s0.astype(F32))


HEADS_PER_STEP = 4


def _stack_maps(q, dqk):
    lane = lax.broadcasted_iota(jnp.int32, q.shape, 1)
    qf = q.astype(F32) * (dqk ** -0.5)
    stacked = jnp.concatenate([jnp.where(lane < dqk, qf, 0.0), jnp.where(lane >= dqk, qf, 0.0)], axis=0)
    return stacked.astype(BF16)


def _lambda(lv_ref, lam_init):
    lv = lv_ref[...]
    return (jnp.exp(jnp.sum(lv[0:1] * lv[1:2], axis=-1, keepdims=True))
            - jnp.exp(jnp.sum(lv[2:3] * lv[3:4], axis=-1, keepdims=True)) + lam_init)


def _diff_attn_kernel(q_ref, k_ref, v_ref, z_ref, lv_ref, gsub_ref, o_ref,
                      qq_s, s_s, p_s, m_s, l_s, a_s, acc_s, *, tq, dqk, n_heads, lam_init):
    qi = pl.program_id(2)
    R = 2 * tq
    hidx = range(n_heads)
    heads = [slice(h * LANES, (h + 1) * LANES) for h in hidx]
    ci = lax.broadcasted_iota(jnp.int32, (tq, R), 0)
    ri = lax.broadcasted_iota(jnp.int32, (tq, R), 1)
    ri = jnp.where(ri >= tq, ri - tq, ri)
    visible = (ci // CHUNK) <= (ri // CHUNK)

    def scores(h, tile_idx):
        k0 = pl.multiple_of(tile_idx * tq, tq)
        return _dot_nt(k_ref[pl.ds(k0, tq), heads[h]], qq_s[h])

    def softmax(h, slot):
        s = s_s[slot, h]
        m = m_s[h]
        m_new = jnp.maximum(m, jnp.max(s, axis=0, keepdims=True))
        alpha = jnp.exp(m - m_new)
        p = jnp.exp(s - m_new)
        l_s[h] = alpha * l_s[h] + jnp.sum(p, axis=0, keepdims=True)
        m_s[h] = m_new
        a_s[slot, h] = alpha
        p_s[slot, h] = p.astype(BF16)

    def values(h, slot, tile_idx):
        k0 = pl.multiple_of(tile_idx * tq, tq)
        acc_s[h] = a_s[slot, h] * acc_s[h] + _dot_tn(v_ref[pl.ds(k0, tq), heads[h]], p_s[slot, h])

    for h in hidx:
        qq_s[h] = _stack_maps(q_ref[:, heads[h]], dqk)
    for h in hidx:
        s_s[0, h] = jnp.where(visible, scores(h, qi), -jnp.inf)
        m_s[h] = jnp.full((1, R), -jnp.inf, F32)
        l_s[h] = jnp.zeros((1, R), F32)
        acc_s[h] = jnp.zeros((LANES, R), F32)
        p_s[1, h] = jnp.zeros((tq, R), BF16)
        a_s[1, h] = jnp.ones((1, R), F32)

    def step(i, carry):
        cur = lax.rem(i, 2)
        nxt = 1 - cur
        prev_tile = jnp.where(i == 1, qi, jnp.maximum(i - 2, 0))
        for h in hidx:
            s_s[nxt, h] = scores(h, i)
        for h in hidx:
            values(h, nxt, prev_tile)
        for h in hidx:
            softmax(h, cur)
        return carry

    lax.fori_loop(0, qi, step, 0)
    cur = lax.rem(qi, 2)
    nxt = 1 - cur
    for h in hidx:
        values(h, nxt, jnp.where(qi == 1, qi, jnp.maximum(qi - 2, 0)))
    for h in hidx:
        softmax(h, cur)
    for h in hidx:
        values(h, cur, jnp.maximum(qi - 1, 0))

    lam = _lambda(lv_ref, lam_init)
    for h in hidx:
        acc = acc_s[h]
        l = l_s[h]
        o = (acc[:, :tq] / l[:, :tq] - lam * (acc[:, tq:] / l[:, tq:])).T
        o_ref[:, heads[h]] = (_rms(o, gsub_ref[...]) * (1.0 - lam_init)
                              * _silu(z_ref[:, heads[h]])).astype(o_ref.dtype)


def _diff_attn(q, k, v, z, lam_vec, g_sub, B, L, lam_init, tq_pref=256):
    dqk = lam_vec.shape[-1]
    HB = q.shape[1] // (2 * dqk)
    dv = v.shape[1] // HB
    assert 2 * dqk == LANES and dv == LANES
    tq = _pick(L, tq_pref, CHUNK)
    assert L % tq == 0 and tq % CHUNK == 0
    NQ = L // tq
    G = min(HEADS_PER_STEP, HB)
    assert HB % G == 0
    NG = HB // G
    W = G * LANES
    qspec = pl.BlockSpec((tq, W), lambda b, g, i: (b * NQ + i, g))
    kspec = pl.BlockSpec((L, W), lambda b, g, i: (b, g), pipeline_mode=pl.Buffered(1))
    return pl.pallas_call(
        functools.partial(_diff_attn_kernel, tq=tq, dqk=dqk, n_heads=G, lam_init=lam_init),
        grid=(B, NG, NQ),
        in_specs=[qspec, kspec, kspec, qspec,
                  pl.BlockSpec(lam_vec.shape, lambda b, g, i: (0, 0)),
                  pl.BlockSpec((1, dv), lambda b, g, i: (0, 0))],
        out_specs=qspec,
        out_shape=jax.ShapeDtypeStruct((B * L, HB * LANES), BF16),
        scratch_shapes=[pltpu.VMEM((G, 2 * tq, LANES), BF16),
                        pltpu.VMEM((2, G, tq, 2 * tq), F32),
                        pltpu.VMEM((2, G, tq, 2 * tq), BF16),
                        pltpu.VMEM((G, 1, 2 * tq), F32),
                        pltpu.VMEM((G, 1, 2 * tq), F32),
                        pltpu.VMEM((2, G, 1, 2 * tq), F32),
                        pltpu.VMEM((G, LANES, 2 * tq), F32)],
        compiler_params=_params(("parallel", "parallel", "arbitrary")),
        name="diff_attn",
    )(q, k, v, z, lam_vec.astype(F32), g_sub.reshape(1, dv).astype(F32))


def _diff_attn_kv_kernel(q_ref, kp_ref, vp_ref, kn_ref, vn_ref, z_ref, lv_ref, gsub_ref, o_ref,
                         *, dqk, lam_init):
    tq = q_ref.shape[0]
    P = kp_ref.shape[0]
    qq = _stack_maps(q_ref[...], dqk)

    def scores(k, kpos0):
        n = k.shape[0]
        ri = lax.broadcasted_iota(jnp.int32, (2 * tq, n), 0)
        ci = lax.broadcasted_iota(jnp.int32, (2 * tq, n), 1)
        qpos = P + jnp.where(ri >= tq, ri - tq, ri)
        visible = ((kpos0 + ci) // CHUNK) <= (qpos // CHUNK)
        return jnp.where(visible, _dot_nt(qq, k.astype(BF16)), -jnp.inf)

    s_p = scores(kp_ref[...], 0)
    s_n = scores(kn_ref[...], P)
    m = jnp.maximum(jnp.max(s_p, axis=-1, keepdims=True), jnp.max(s_n, axis=-1, keepdims=True))
    p_p = jnp.exp(s_p - m)
    p_n = jnp.exp(s_n - m)
    l = jnp.sum(p_p, axis=-1, keepdims=True) + jnp.sum(p_n, axis=-1, keepdims=True)
    acc = _dot(p_p.astype(BF16), vp_ref[...].astype(BF16)) + _dot(p_n.astype(BF16), vn_ref[...].astype(BF16))
    o = acc[:tq] / l[:tq] - _lambda(lv_ref, lam_init) * (acc[tq:] / l[tq:])
    o_ref[...] = (_rms(o, gsub_ref[...]) * (1.0 - lam_init) * _silu(z_ref[...])).astype(o_ref.dtype)


def _diff_attn_kv(q, k_past, v_past, k_new, v_new, z, lam_vec, g_sub, B, L, lam_init):
    dqk = lam_vec.shape[-1]
    HB = q.shape[1] // (2 * dqk)
    dv = v_new.shape[1] // HB
    P = k_past.shape[1]
    assert 2 * dqk == LANES and dv == LANES
    qspec = pl.BlockSpec((L, LANES), lambda b, h: (b, h))
    pspec = pl.BlockSpec((None, P, LANES), lambda b, h: (b, 0, h))
    return pl.pallas_call(
        functools.partial(_diff_attn_kv_kernel, dqk=dqk, lam_init=lam_init),
        grid=(B, HB),
        in_specs=[qspec, pspec, pspec, qspec, qspec, qspec,
                  pl.BlockSpec(lam_vec.shape, lambda b, h: (0, 0)),
                  pl.BlockSpec((1, dv), lambda b, h: (0, 0))],
        out_specs=qspec,
        out_shape=jax.ShapeDtypeStruct((B * L, HB * dv), BF16),
        compiler_params=_params(("parallel", "parallel")),
        name="diff_attn_kv",
    )(q, k_past, v_past, k_new, v_new, z, lam_vec.astype(F32), g_sub.reshape(1, dv).astype(F32))


def _mem_attn_kernel(q_ref, mk_ref, mv_ref, z_ref, o_ref, *, scale):
    s = _dot_nt(q_ref[...], mk_ref[...].astype(BF16)) * scale
    p = jnp.exp(s - jnp.max(s, axis=-1, keepdims=True))
    o = _dot(p.astype(BF16), mv_ref[...].astype(BF16)) / jnp.sum(p, axis=-1, keepdims=True)
    o_ref[...] = (o * _silu(z_ref[...])).astype(o_ref.dtype)


def _mem_attn(q, mk, mv, z, B, L, n_mem, tq_pref=1024):
    dh = LANES
    HC = q.shape[1] // dh
    tq = _pick(L, tq_pref, 16)
    NQ = L // tq
    qspec = pl.BlockSpec((tq, dh), lambda b, h, i: (b * NQ + i, h))
    mspec = pl.BlockSpec((n_mem, dh), lambda b, h, i: (b, h))
    return pl.pallas_call(
        functools.partial(_mem_attn_kernel, scale=dh ** -0.5),
        grid=(B, HC, NQ),
        in_specs=[qspec, mspec, mspec, qspec],
        out_specs=qspec,
        out_shape=jax.ShapeDtypeStruct((B * L, HC * dh), BF16),
        compiler_params=_params(("parallel", "parallel", "parallel")),
        name="mem_attn",
    )(q, mk, mv, z)


def _out_proj_kernel(oa_ref, ob_ref, oc_ref, x_ref, w_ref, g_ref, y_ref, *, final_norm):
    wa, wb = oa_ref.shape[1], ob_ref.shape[1]
    y = (x_ref[...] + _dot(oa_ref[...], w_ref[0:wa, :]) + _dot(ob_ref[...], w_ref[wa:wa + wb, :])
         + _dot(oc_ref[...], w_ref[wa + wb:, :]))
    y_ref[...] = _rms(y, g_ref[...]) if final_norm else y


def _out_proj(o_a, o_b, o_c, x2d, w_bf16, g_final, final_norm, tm_pref=512):
    T, D = x2d.shape
    tm = _pick(T, tm_pref, 16)
    rows = lambda w: pl.BlockSpec((tm, w), lambda i: (i, 0))
    return pl.pallas_call(
        functools.partial(_out_proj_kernel, final_norm=final_norm),
        grid=(T // tm,),
        in_specs=[rows(o_a.shape[1]), rows(o_b.shape[1]), rows(o_c.shape[1]), rows(D),
                  pl.BlockSpec(w_bf16.shape, lambda i: (0, 0)),
                  pl.BlockSpec((1, D), lambda i: (0, 0))],
        out_specs=rows(D),
        out_shape=jax.ShapeDtypeStruct((T, D), F32),
        compiler_params=_params(("parallel",)),
        name="out_proj",
    )(o_a, o_b, o_c, x2d, w_bf16, g_final.reshape(1, D).astype(F32))


def _split_w_in(w_in, sizes):
    offs = [0]
    for s in sizes:
        offs.append(offs[-1] + s)
    return [w_in[:, offs[i]:offs[i + 1]] for i in range(len(sizes))]


def _layer(x, k_past, v_past, mem_k, mem_v, conv_init, s0, w_in_perm, conv_w, a_log, dt_bias, g_pre,
           g_gdn, lam_vec, g_sub, w_out_bf16, g_final, lam_init, final_norm, dims):
    B, L, D = x.shape
    QKV_A, WA, WB, WC, n_mem = dims
    x2d = x.reshape(B * L, D)
    widths = (QKV_A, WA, WB, WB, WB, WC, WB, WC, LANES)
    offs = [0]
    for w in widths:
        offs.append(offs[-1] + w)
    groups = ((offs[0], QKV_A, (0,)), (offs[1], WA, (1,)), (offs[2], WB, (2, 3)), (offs[3], WB, (4, 5)),
              (offs[4], WB, (6,)), (offs[5], WC, (7,)), (offs[6], WB, (8,)), (offs[7], WC, (9,)),
              (offs[8], LANES, (10,)))
    out_defs = ((QKV_A, F32), (WA, F32), (WB, F32), (WB, BF16), (WB, F32), (WB, BF16), (WB, F32),
                (WC, F32), (WB, BF16), (WC, BF16), (LANES, F32))
    (qkv_a, z_a, k_b, k_b16, v_b, v_b16, z_b, z_c, q_b16, q_c16, ba) = _norm_matmul(
        x2d, g_pre, w_in_perm, groups, out_defs)

    o_a, s_new = _gdn(qkv_a, z_a, ba, conv_w, conv_init, s0, a_log, dt_bias, g_gdn, B, L)
    if k_past is None:
        o_b = _diff_attn(q_b16, k_b16, v_b16, z_b, lam_vec, g_sub, B, L, lam_init)
    else:
        o_b = _diff_attn_kv(q_b16, k_past, v_past, k_b16, v_b16, z_b, lam_vec, g_sub, B, L, lam_init)
    o_c = _mem_attn(q_c16, mem_k, mem_v, z_c, B, L, n_mem)
    y = _out_proj(o_a, o_b, o_c, x2d, w_out_bf16, g_final, final_norm).reshape(B, L, D)
    conv_new = qkv_a.reshape(B, L, QKV_A)[:, L - (conv_w.shape[0] - 1):, :]
    return y, k_b, v_b, conv_new, s_new


def kernel(x_prompt, x_sample, mem_prompt, cache_diff_k, cache_diff_v, cache_mem_k, cache_mem_v,
           state_gdn, cache_conv, g_pre, w_in, conv_w, a_log, dt_bias, g_gdn, lam_vec, g_sub, g_mem,
           w_mem_kv, w_out, g_final):
    depth = w_in.shape[0]
    B, L, D = x_prompt.shape
    Bs, Ls, _ = x_sample.shape
    _, _, P, HB, _, DQK = cache_diff_k.shape
    DV_B = cache_diff_v.shape[-1]
    _, _, n_mem, HC, DH_C = cache_mem_k.shape
    _, _, HA, DK_A, DV_A = state_gdn.shape
    CW = conv_w.shape[1]
    QKV_A = conv_w.shape[2]
    WA, WB, WC = HA * DV_A, HB * DV_B, HC * DH_C
    assert QKV_A == HA * (2 * DK_A + DV_A) and DK_A == DV_A == LANES and DH_C == LANES
    in_sizes = (QKV_A, WA, HA, HA, HB * 2 * DQK, HB * 2 * DQK, WB, WB, WC, WC)
    assert w_in.shape[2] == sum(in_sizes) and 2 * HA <= LANES
    dims = (QKV_A, WA, WB, WC, n_mem)

    hp, hs = x_prompt, x_sample
    outs = [[] for _ in range(10)]
    for i in range(depth):
        lam_init = 0.8 - 0.6 * math.exp(-0.3 * i)
        final_norm = i == depth - 1
        qkv_w, za_w, b_w, a_w, qb_w, kb_w, vb_w, zb_w, qc_w, zc_w = _split_w_in(w_in[i], in_sizes)
        ba_w = jnp.concatenate([b_w, a_w, jnp.zeros((D, LANES - 2 * HA), w_in.dtype)], axis=1)
        w_in_perm = jnp.concatenate(
            [qkv_w, za_w, kb_w, vb_w, zb_w, zc_w, qb_w, qc_w, ba_w], axis=1).astype(BF16)
        w_out_bf16 = w_out[i].astype(BF16)

        mk, mk16, mv, mv16 = _norm_matmul(
            mem_prompt.reshape(B * n_mem, D), g_mem[i], w_mem_kv[i].astype(BF16),
            ((0, WC, (0, 1)), (WC, WC, (2, 3))), ((WC, F32), (WC, BF16), (WC, F32), (WC, BF16)))

        wts = (w_in_perm, conv_w[i], a_log[i], dt_bias[i], g_pre[i], g_gdn[i], lam_vec[i], g_sub[i],
               w_out_bf16, g_final, lam_init, final_norm, dims)
        hp, kp, vp, cp, sp = _layer(
            hp, None, None, mk16, mv16, jnp.zeros((B, CW - 1, QKV_A), F32),
            jnp.zeros((B, HA, DK_A, DV_A), F32), *wts)
        hs, kn, vn, cn, sn = _layer(
            hs, cache_diff_k[i].reshape(Bs, P, HB * 2 * DQK), cache_diff_v[i].reshape(Bs, P, WB),
            cache_mem_k[i].reshape(Bs * n_mem, WC), cache_mem_v[i].reshape(Bs * n_mem, WC),
            cache_conv[i], state_gdn[i], *wts)
        layer_out = (kp.reshape(B, L, HB, 2, DQK), vp.reshape(B, L, HB, DV_B),
                     mk.reshape(B, n_mem, HC, DH_C), mv.reshape(B, n_mem, HC, DH_C), sp, cp,
                     kn.reshape(Bs, Ls, HB, 2, DQK), vn.reshape(Bs, Ls, HB, DV_B), sn, cn)
        for lst, o in zip(outs, layer_out):
            lst.append(o)
    return (hp, hs) + tuple(jnp.stack(o) for o in outs)
```

```python
import functools
import math

import jax
import jax.numpy as jnp
from jax import lax
from jax.experimental import pallas as pl
from jax.experimental.pallas import tpu as pltpu

F32 = jnp.float32
BF16 = jnp.bfloat16
EPS = 1e-6
CHUNK = 64
LANES = 128
VMEM_LIMIT = 56 * 1024 * 1024
HIGHEST = lax.Precision.HIGHEST


def _pick(n, pref, align):
    if n <= pref:
        return n
    t = (pref // align) * align
    while t > align and n % t:
        t -= align
    assert n % t == 0, (n, pref, align)
    return t


def _sigmoid(x):
    return 1.0 / (1.0 + jnp.exp(-x))


def _silu(x):
    return x * _sigmoid(x)


def _softplus(x):
    return jnp.maximum(x, 0.0) + jnp.log(1.0 + jnp.exp(-jnp.abs(x)))


def _rms(x, g):
    return x * lax.rsqrt(jnp.mean(x * x, axis=-1, keepdims=True) + EPS) * g


def _dot(a, b):
    return jnp.dot(a, b, preferred_element_type=F32)


def _dot_nt(a, b):
    return lax.dot_general(a, b, (((1,), (1,)), ((), ())), preferred_element_type=F32)


def _dot_tn(a, b):
    return lax.dot_general(a, b, (((0,), (0,)), ((), ())), preferred_element_type=F32)


def _params(sem):
    return pltpu.CompilerParams(dimension_semantics=sem, vmem_limit_bytes=VMEM_LIMIT)


COL_CHUNK = 512


def _norm_matmul_kernel(x_ref, g_ref, w_ref, *out_refs, groups):
    hb = _rms(x_ref[...], g_ref[...]).astype(BF16)
    for w_off, width, outs in groups:
        for c0 in range(0, width, COL_CHUNK):
            cw = min(COL_CHUNK, width - c0)
            r = _dot(hb, w_ref[:, w_off + c0:w_off + c0 + cw])
            for oi in outs:
                out_refs[oi][:, c0:c0 + cw] = r.astype(out_refs[oi].dtype)


def _norm_matmul(x2d, g, w_bf16, groups, out_defs, tm_pref=256):
    T, D = x2d.shape
    tm = _pick(T, tm_pref, 16)
    return pl.pallas_call(
        functools.partial(_norm_matmul_kernel, groups=groups),
        grid=(T // tm,),
        in_specs=[
            pl.BlockSpec((tm, D), lambda i: (i, 0)),
            pl.BlockSpec((1, D), lambda i: (0, 0)),
            pl.BlockSpec(w_bf16.shape, lambda i: (0, 0), pipeline_mode=pl.Buffered(1)),
        ],
        out_specs=[pl.BlockSpec((tm, w), lambda i: (i, 0)) for w, _ in out_defs],
        out_shape=[jax.ShapeDtypeStruct((T, w), dt) for w, dt in out_defs],
        compiler_params=_params(("parallel",)),
        name="norm_matmul",
    )(x2d, g.reshape(1, D), w_bf16)


CONV_PAD = 8
GDN_HEADS = 4


def _split3(x):
    hi = x.astype(BF16)
    r1 = x - hi.astype(F32)
    mid = r1.astype(BF16)
    lo = (r1 - mid.astype(F32)).astype(BF16)
    return hi, mid, lo


def _gdn_kernel(q_ref, k_ref, v_ref, cwq_ref, cwk_ref, cwv_ref, ciq_ref, cik_ref, civ_ref,
                ba_ref, z_ref, alog_ref, dtb_ref, ggdn_ref, s0_ref,
                o_ref, sout_ref,
                xq, xk, xv, S_s,
                *, C, ST, TT, conv_w, n_heads, G):
    hg = pl.program_id(1)
    t = pl.program_id(2)
    nt = pl.num_programs(2)
    DK = LANES
    lead = CONV_PAD - (conv_w - 1)
    n_sub = TT // ST
    n_chunk = ST // C
    heads = [slice(j * DK, (j + 1) * DK) for j in range(G)]

    @pl.when(t == 0)
    def _():
        xq[0:CONV_PAD, :] = ciq_ref[...]
        xk[0:CONV_PAD, :] = cik_ref[...]
        xv[0:CONV_PAD, :] = civ_ref[...]
        S_s[...] = s0_ref[...]

    @pl.when(t > 0)
    def _():
        xq[0:CONV_PAD, :] = xq[TT:TT + CONV_PAD, :]
        xk[0:CONV_PAD, :] = xk[TT:TT + CONV_PAD, :]
        xv[0:CONV_PAD, :] = xv[TT:TT + CONV_PAD, :]

    xq[CONV_PAD:, :] = q_ref[...]
    xk[CONV_PAD:, :] = k_ref[...]
    xv[CONV_PAD:, :] = v_ref[...]

    def conv_silu(xbuf, cw_ref, hs):
        w = cw_ref[:, hs]
        y = xbuf[lead:lead + TT, hs] * w[0:1, :]
        for i in range(1, conv_w):
            y = y + xbuf[lead + i:lead + i + TT, hs] * w[i:i + 1, :]
        return _silu(y)

    def l2n(y):
        return y * lax.rsqrt(jnp.sum(y * y, axis=-1, keepdims=True) + EPS)

    ri = lax.broadcasted_iota(jnp.int32, (ST, ST), 0)
    ci = lax.broadcasted_iota(jnp.int32, (ST, ST), 1)
    same = (ri // C) == (ci // C)
    mask_c = same & (ci <= ri)
    mask_s = same & (ci < ri)
    tri = jnp.where(mask_c, 1.0, 0.0).astype(BF16)
    n_sq = int(math.log2(C)) - 1

    ba = ba_ref[...]
    lane = lax.broadcasted_iota(jnp.int32, ba.shape, 1)
    gate = jnp.where(lane < n_heads, _sigmoid(ba),
                     -jnp.exp(alog_ref[...]) * _softplus(ba + dtb_ref[...]))
    g3 = jnp.concatenate(_split3(gate), axis=-1)
    cums = []
    for s in range(n_sub):
        c3 = _dot(tri, g3[s * ST:(s + 1) * ST])
        cums.append(c3[:, :LANES] + c3[:, LANES:2 * LANES] + c3[:, 2 * LANES:])
    cum = cums[0] if n_sub == 1 else jnp.concatenate(cums, axis=0)
    gc3 = jnp.concatenate(_split3(jnp.concatenate([gate, cum], axis=-1)), axis=0)
    rowi = lax.broadcasted_iota(jnp.int32, (2 * LANES, 2 * LANES), 0)
    coli = lax.broadcasted_iota(jnp.int32, (2 * LANES, 2 * LANES), 1)

    qkvs = []
    for j in range(G):
        hs = heads[j]
        qkvs.append((l2n(conv_silu(xq, cwq_ref, hs)) * (DK ** -0.5), l2n(conv_silu(xk, cwk_ref, hs)),
                     conv_silu(xv, cwv_ref, hs)))

    pairs = [(j, s) for s in range(n_sub) for j in range(G)]
    rows = {p: slice(p[1] * ST, (p[1] + 1) * ST) for p in pairs}
    reps = {}
    for j, s in pairs:
        h = hg * G + j
        lo_r = s * ST
        sel = jnp.where(rowi == jnp.where(coli < LANES, h, LANES + n_heads + h), 1.0, 0.0).astype(BF16)
        reps[j, s] = _dot(jnp.concatenate([gc3[i * TT + lo_r:i * TT + lo_r + ST] for i in range(3)], axis=0), sel)
    qkk = {p: _dot_nt(jnp.concatenate([qkvs[p[0]][0][rows[p]].astype(BF16),
                                       qkvs[p[0]][1][rows[p]].astype(BF16)], axis=0),
                      qkvs[p[0]][1][rows[p]].astype(BF16)) for p in pairs}
    beta, Gc, eG, attn, Qm, Mb = {}, {}, {}, {}, {}, {}
    for p in pairs:
        rep = reps[p][:ST] + reps[p][ST:2 * ST] + reps[p][2 * ST:]
        beta[p] = rep[:, :LANES]
        Gc[p] = rep[:, LANES:]
        eG[p] = jnp.exp(Gc[p])
        decay = jnp.exp(jnp.where(mask_c, Gc[p][:, :ST] - Gc[p].T[:ST, :], -jnp.inf))
        attn[p] = (qkk[p][:ST] * decay).astype(BF16)
        lmat = jnp.where(mask_s, beta[p][:, :ST] * qkk[p][ST:] * decay, 0.0)
        Qm[p] = -lmat
        Mb[p] = lmat.astype(BF16)
    for _ in range(n_sq):
        M = {p: _dot(Mb[p], Mb[p]) for p in pairs}
        Mb = {p: M[p].astype(BF16) for p in pairs}
        QM = {p: _dot(Qm[p].astype(BF16), Mb[p]) for p in pairs}
        Qm = {p: Qm[p] + M[p] + QM[p] for p in pairs}
    rhs = {}
    for p in pairs:
        q, k, v = (a[rows[p]] for a in qkvs[p[0]])
        rhs[p] = jnp.concatenate([v * beta[p], k * (beta[p] * eG[p])], axis=-1)
    corr = {p: _dot(Qm[p].astype(BF16), rhs[p].astype(BF16)) for p in pairs}
    chunk = {}
    for p in pairs:
        q, k, v = (a[rows[p]] for a in qkvs[p[0]])
        sol = rhs[p] + corr[p]
        for c in range(n_chunk):
            lo, hi = c * C, (c + 1) * C
            g_last = Gc[p][hi - 1:hi, :]
            k_tail = (k[lo:hi] * jnp.exp(g_last - Gc[p][lo:hi])).astype(BF16)
            wq = jnp.concatenate([sol[lo:hi, DK:], q[lo:hi] * eG[p][lo:hi]], axis=0).astype(BF16)
            chunk[p, c] = (sol[lo:hi, :DK], wq, attn[p][lo:hi, :], k_tail, jnp.exp(g_last))

    outs = [[] for _ in range(G)]
    for s in range(n_sub):
        for c in range(n_chunk):
            S = [S_s[j] for j in range(G)]
            r = [_dot(chunk[(j, s), c][1], S[j].astype(BF16)) for j in range(G)]
            eb = [(chunk[(j, s), c][0] - r[j][:C]).astype(BF16) for j in range(G)]
            for j in range(G):
                U, wq, attn_c, k_tail, gl = chunk[(j, s), c]
                parts = [eb[j] if i == c else jnp.zeros_like(eb[j]) for i in range(n_chunk)]
                e_pad = parts[0] if n_chunk == 1 else jnp.concatenate(parts, axis=0)
                outs[j].append(r[j][C:] + _dot(attn_c, e_pad))
                S_s[j] = S[j] * gl + _dot_tn(k_tail, eb[j])

    for j in range(G):
        o = outs[j][0] if len(outs[j]) == 1 else jnp.concatenate(outs[j], axis=0)
        o_ref[:, heads[j]] = (_rms(o, ggdn_ref[...]) * _silu(z_ref[:, heads[j]])).astype(o_ref.dtype)

    @pl.when(t == nt - 1)
    def _():
        sout_ref[...] = S_s[...]


def _gdn(qkv, z_a, ba, conv_w, conv_init, s0, a_log, dt_bias, g_gdn, B, L, tt_pref=512):
    H = a_log.shape[0]
    DK = qkv.shape[1] // (3 * H)
    assert DK == LANES and s0.shape == (B, H, DK, DK)
    CW = conv_w.shape[0]
    C = min(CHUNK, L)
    TT = _pick(L, tt_pref, C)
    ST = min(TT, 2 * C)
    G = min(GDN_HEADS, H)
    assert L % C == 0 and TT % ST == 0 and ST % C == 0 and C & (C - 1) == 0 and H % G == 0
    NT = L // TT
    NG = H // G
    W = G * DK
    cinit = jnp.concatenate(
        [jnp.zeros((B, CONV_PAD - (CW - 1), qkv.shape[1]), F32), conv_init.astype(F32)], axis=1)
    pad = jnp.zeros((LANES - 2 * H,), F32)
    alog_v = jnp.concatenate([jnp.zeros((H,), F32), a_log.astype(F32), pad]).reshape(1, LANES)
    dtb_v = jnp.concatenate([jnp.zeros((H,), F32), dt_bias.astype(F32), pad]).reshape(1, LANES)

    def row(b, g, t):
        return b * NT + t

    tile = lambda off: pl.BlockSpec((TT, W), lambda b, g, t: (row(b, g, t), off * NG + g))
    cws = lambda off: pl.BlockSpec((CW, W), lambda b, g, t: (0, off * NG + g))
    cis = lambda off: pl.BlockSpec((None, CONV_PAD, W), lambda b, g, t: (b, 0, off * NG + g))
    full = lambda shp: pl.BlockSpec(shp, lambda b, g, t: (0, 0))
    st_spec = pl.BlockSpec((None, G, DK, DK), lambda b, g, t: (b, g, 0, 0))
    return pl.pallas_call(
        functools.partial(_gdn_kernel, C=C, ST=ST, TT=TT, conv_w=CW, n_heads=H, G=G),
        grid=(B, NG, NT),
        in_specs=[tile(0), tile(1), tile(2), cws(0), cws(1), cws(2), cis(0), cis(1), cis(2),
                  pl.BlockSpec((TT, LANES), lambda b, g, t: (row(b, g, t), 0)),
                  pl.BlockSpec((TT, W), lambda b, g, t: (row(b, g, t), g)),
                  full((1, LANES)), full((1, LANES)), full((1, DK)), st_spec],
        out_specs=[pl.BlockSpec((TT, W), lambda b, g, t: (row(b, g, t), g)), st_spec],
        out_shape=[jax.ShapeDtypeStruct((B * L, H * DK), BF16),
                   jax.ShapeDtypeStruct((B, H, DK, DK), F32)],
        scratch_shapes=[pltpu.VMEM((TT + CONV_PAD, W), F32)] * 3 + [pltpu.VMEM((G, DK, DK), F32)],
        compiler_params=_params(("parallel", "parallel", "arbitrary")),
        name="gdn",
    )(qkv, qkv, qkv, conv_w, conv_w, conv_w, cinit, cinit, cinit, ba, z_a,
      alog_v, dtb_v, g_gdn.reshape(1, DK).astype(F32), s0.astype(F32))


HEADS_PER_STEP = 4


def _stack_maps(q, dqk):
    lane = lax.broadcasted_iota(jnp.int32, q.shape, 1)
    qf = q.astype(F32) * (dqk ** -0.5)
    stacked = jnp.concatenate([jnp.where(lane < dqk, qf, 0.0), jnp.where(lane >= dqk, qf, 0.0)], axis=0)
    return stacked.astype(BF16)


def _lambda(lv_ref, lam_init):
    lv = lv_ref[...]
    return (jnp.exp(jnp.sum(lv[0:1] * lv[1:2], axis=-1, keepdims=True))
            - jnp.exp(jnp.sum(lv[2:3] * lv[3:4], axis=-1, keepdims=True)) + lam_init)


def _diff_attn_kernel(q_ref, k_ref, v_ref, z_ref, lv_ref, gsub_ref, o_ref,
                      qq_s, s_s, p_s, m_s, l_s, a_s, acc_s, *, tq, dqk, n_heads, lam_init):
    qi = pl.program_id(2)
    R = 2 * tq
    hidx = range(n_heads)
    heads = [slice(h * LANES, (h + 1) * LANES) for h in hidx]
    ci = lax.broadcasted_iota(jnp.int32, (tq, R), 0)
    ri = lax.broadcasted_iota(jnp.int32, (tq, R), 1)
    ri = jnp.where(ri >= tq, ri - tq, ri)
    visible = (ci // CHUNK) <= (ri // CHUNK)

    def scores(h, tile_idx):
        k0 = pl.multiple_of(tile_idx * tq, tq)
        return _dot_nt(k_ref[pl.ds(k0, tq), heads[h]], qq_s[h])

    def softmax(h, slot):
        s = s_s[slot, h]
        m = m_s[h]
        m_new = jnp.maximum(m, jnp.max(s, axis=0, keepdims=True))
        alpha = jnp.exp(m - m_new)
        p = jnp.exp(s - m_new)
        l_s[h] = alpha * l_s[h] + jnp.sum(p, axis=0, keepdims=True)
        m_s[h] = m_new
        a_s[slot, h] = alpha
        p_s[slot, h] = p.astype(BF16)

    def values(h, slot, tile_idx):
        k0 = pl.multiple_of(tile_idx * tq, tq)
        acc_s[h] = a_s[slot, h] * acc_s[h] + _dot_tn(v_ref[pl.ds(k0, tq), heads[h]], p_s[slot, h])

    for h in hidx:
        qq_s[h] = _stack_maps(q_ref[:, heads[h]], dqk)
    for h in hidx:
        s_s[0, h] = jnp.where(visible, scores(h, qi), -jnp.inf)
        m_s[h] = jnp.full((1, R), -jnp.inf, F32)
        l_s[h] = jnp.zeros((1, R), F32)
        acc_s[h] = jnp.zeros((LANES, R), F32)
        p_s[1, h] = jnp.zeros((tq, R), BF16)
        a_s[1, h] = jnp.ones((1, R), F32)

    def step(i, carry):
        cur = lax.rem(i, 2)
        nxt = 1 - cur
        prev_tile = jnp.where(i == 1, qi, jnp.maximum(i - 2, 0))
        for h in hidx:
            s_s[nxt, h] = scores(h, i)
        for h in hidx:
            values(h, nxt, prev_tile)
        for h in hidx:
            softmax(h, cur)
        return carry

    lax.fori_loop(0, qi, step, 0)
    cur = lax.rem(qi, 2)
    nxt = 1 - cur
    for h in hidx:
        values(h, nxt, jnp.where(qi == 1, qi, jnp.maximum(qi - 2, 0)))
    for h in hidx:
        softmax(h, cur)
    for h in hidx:
        values(h, cur, jnp.maximum(qi - 1, 0))

    lam = _lambda(lv_ref, lam_init)
    for h in hidx:
        acc = acc_s[h]
        l = l_s[h]
        o = (acc[:, :tq] / l[:, :tq] - lam * (acc[:, tq:] / l[:, tq:])).T
        o_ref[:, heads[h]] = (_rms(o, gsub_ref[...]) * (1.0 - lam_init)
                              * _silu(z_ref[:, heads[h]])).astype(o_ref.dtype)


def _diff_attn(q, k, v, z, lam_vec, g_sub, B, L, lam_init, tq_pref=256):
    dqk = lam_vec.shape[-1]
    HB = q.shape[1] // (2 * dqk)
    dv = v.shape[1] // HB
    assert 2 * dqk == LANES and dv == LANES
    tq = _pick(L, tq_pref, CHUNK)
    assert L % tq == 0 and tq % CHUNK == 0
    NQ = L // tq
    G = min(HEADS_PER_STEP, HB)
    assert HB % G == 0
    NG = HB // G
    W = G * LANES
    qspec = pl.BlockSpec((tq, W), lambda b, g, i: (b * NQ + i, g))
    kspec = pl.BlockSpec((L, W), lambda b, g, i: (b, g), pipeline_mode=pl.Buffered(1))
    return pl.pallas_call(
        functools.partial(_diff_attn_kernel, tq=tq, dqk=dqk, n_heads=G, lam_init=lam_init),
        grid=(B, NG, NQ),
        in_specs=[qspec, kspec, kspec, qspec,
                  pl.BlockSpec(lam_vec.shape, lambda b, g, i: (0, 0)),
                  pl.BlockSpec((1, dv), lambda b, g, i: (0, 0))],
        out_specs=qspec,
        out_shape=jax.ShapeDtypeStruct((B * L, HB * LANES), BF16),
        scratch_shapes=[pltpu.VMEM((G, 2 * tq, LANES), BF16),
                        pltpu.VMEM((2, G, tq, 2 * tq), F32),
                        pltpu.VMEM((2, G, tq, 2 * tq), BF16),
                        pltpu.VMEM((G, 1, 2 * tq), F32),
                        pltpu.VMEM((G, 1, 2 * tq), F32),
                        pltpu.VMEM((2, G, 1, 2 * tq), F32),
                        pltpu.VMEM((G, LANES, 2 * tq), F32)],
        compiler_params=_params(("parallel", "parallel", "arbitrary")),
        name="diff_attn",
    )(q, k, v, z, lam_vec.astype(F32), g_sub.reshape(1, dv).astype(F32))


def _diff_attn_kv_kernel(q_ref, kp_ref, vp_ref, kn_ref, vn_ref, z_ref, lv_ref, gsub_ref, o_ref,
                         *, dqk, lam_init):
    tq = q_ref.shape[0]
    P = kp_ref.shape[0]
    qq = _stack_maps(q_ref[...], dqk)

    def scores(k, kpos0):
        n = k.shape[0]
        ri = lax.broadcasted_iota(jnp.int32, (2 * tq, n), 0)
        ci = lax.broadcasted_iota(jnp.int32, (2 * tq, n), 1)
        qpos = P + jnp.where(ri >= tq, ri - tq, ri)
        visible = ((kpos0 + ci) // CHUNK) <= (qpos // CHUNK)
        return jnp.where(visible, _dot_nt(qq, k.astype(BF16)), -jnp.inf)

    s_p = scores(kp_ref[...], 0)
    s_n = scores(kn_ref[...], P)
    m = jnp.maximum(jnp.max(s_p, axis=-1, keepdims=True), jnp.max(s_n, axis=-1, keepdims=True))
    p_p = jnp.exp(s_p - m)
    p_n = jnp.exp(s_n - m)
    l = jnp.sum(p_p, axis=-1, keepdims=True) + jnp.sum(p_n, axis=-1, keepdims=True)
    acc = _dot(p_p.astype(BF16), vp_ref[...].astype(BF16)) + _dot(p_n.astype(BF16), vn_ref[...].astype(BF16))
    o = acc[:tq] / l[:tq] - _lambda(lv_ref, lam_init) * (acc[tq:] / l[tq:])
    o_ref[...] = (_rms(o, gsub_ref[...]) * (1.0 - lam_init) * _silu(z_ref[...])).astype(o_ref.dtype)


def _diff_attn_kv(q, k_past, v_past, k_new, v_new, z, lam_vec, g_sub, B, L, lam_init):
    dqk = lam_vec.shape[-1]
    HB = q.shape[1] // (2 * dqk)
    dv = v_new.shape[1] // HB
    P = k_past.shape[1]
    assert 2 * dqk == LANES and dv == LANES
    qspec = pl.BlockSpec((L, LANES), lambda b, h: (b, h))
    pspec = pl.BlockSpec((None, P, LANES), lambda b, h: (b, 0, h))
    return pl.pallas_call(
        functools.partial(_diff_attn_kv_kernel, dqk=dqk, lam_init=lam_init),
        grid=(B, HB),
        in_specs=[qspec, pspec, pspec, qspec, qspec, qspec,
                  pl.BlockSpec(lam_vec.shape, lambda b, h: (0, 0)),
                  pl.BlockSpec((1, dv), lambda b, h: (0, 0))],
        out_specs=qspec,
        out_shape=jax.ShapeDtypeStruct((B * L, HB * dv), BF16),
        compiler_params=_params(("parallel", "parallel")),
        name="diff_attn_kv",
    )(q, k_past, v_past, k_new, v_new, z, lam_vec.astype(F32), g_sub.reshape(1, dv).astype(F32))


def _mem_attn_kernel(q_ref, mk_ref, mv_ref, z_ref, o_ref, *, scale):
    s = _dot_nt(q_ref[...], mk_ref[...].astype(BF16)) * scale
    p = jnp.exp(s - jnp.max(s, axis=-1, keepdims=True))
    o = _dot(p.astype(BF16), mv_ref[...].astype(BF16)) / jnp.sum(p, axis=-1, keepdims=True)
    o_ref[...] = (o * _silu(z_ref[...])).astype(o_ref.dtype)


def _mem_attn(q, mk, mv, z, B, L, n_mem, tq_pref=1024):
    dh = LANES
    HC = q.shape[1] // dh
    tq = _pick(L, tq_pref, 16)
    NQ = L // tq
    qspec = pl.BlockSpec((tq, dh), lambda b, h, i: (b * NQ + i, h))
    mspec = pl.BlockSpec((n_mem, dh), lambda b, h, i: (b, h))
    return pl.pallas_call(
        functools.partial(_mem_attn_kernel, scale=dh ** -0.5),
        grid=(B, HC, NQ),
        in_specs=[qspec, mspec, mspec, qspec],
        out_specs=qspec,
        out_shape=jax.ShapeDtypeStruct((B * L, HC * dh), BF16),
        compiler_params=_params(("parallel", "parallel", "parallel")),
        name="mem_attn",
    )(q, mk, mv, z)


def _out_proj_kernel(oa_ref, ob_ref, oc_ref, x_ref, w_ref, g_ref, y_ref, *, final_norm):
    wa, wb = oa_ref.shape[1], ob_ref.shape[1]
    y = (x_ref[...] + _dot(oa_ref[...], w_ref[0:wa, :]) + _dot(ob_ref[...], w_ref[wa:wa + wb, :])
         + _dot(oc_ref[...], w_ref[wa + wb:, :]))
    y_ref[...] = _rms(y, g_ref[...]) if final_norm else y


def _out_proj(o_a, o_b, o_c, x2d, w_bf16, g_final, final_norm, tm_pref=512):
    T, D = x2d.shape
    tm = _pick(T, tm_pref, 16)
    rows = lambda w: pl.BlockSpec((tm, w), lambda i: (i, 0))
    return pl.pallas_call(
        functools.partial(_out_proj_kernel, final_norm=final_norm),
        grid=(T // tm,),
        in_specs=[rows(o_a.shape[1]), rows(o_b.shape[1]), rows(o_c.shape[1]), rows(D),
                  pl.BlockSpec(w_bf16.shape, lambda i: (0, 0)),
                  pl.BlockSpec((1, D), lambda i: (0, 0))],
        out_specs=rows(D),
        out_shape=jax.ShapeDtypeStruct((T, D), F32),
        compiler_params=_params(("parallel",)),
        name="out_proj",
    )(o_a, o_b, o_c, x2d, w_bf16, g_final.reshape(1, D).astype(F32))


def _split_w_in(w_in, sizes):
    offs = [0]
    for s in sizes:
        offs.append(offs[-1] + s)
    return [w_in[:, offs[i]:offs[i + 1]] for i in range(len(sizes))]


def _layer(x, k_past, v_past, mem_k, mem_v, conv_init, s0, w_in_perm, conv_w, a_log, dt_bias, g_pre,
           g_gdn, lam_vec, g_sub, w_out_bf16, g_final, lam_init, final_norm, dims):
    B, L, D = x.shape
    QKV_A, WA, WB, WC, n_mem = dims
    x2d = x.reshape(B * L, D)
    widths = (QKV_A, WA, WB, WB, WB, WC, WB, WC, LANES)
    offs = [0]
    for w in widths:
        offs.append(offs[-1] + w)
    groups = ((offs[0], QKV_A, (0,)), (offs[1], WA, (1,)), (offs[2], WB, (2, 3)), (offs[3], WB, (4, 5)),
              (offs[4], WB, (6,)), (offs[5], WC, (7,)), (offs[6], WB, (8,)), (offs[7], WC, (9,)),
              (offs[8], LANES, (10,)))
    out_defs = ((QKV_A, F32), (WA, F32), (WB, F32), (WB, BF16), (WB, F32), (WB, BF16), (WB, F32),
                (WC, F32), (WB, BF16), (WC, BF16), (LANES, F32))
    (qkv_a, z_a, k_b, k_b16, v_b, v_b16, z_b, z_c, q_b16, q_c16, ba) = _norm_matmul(
        x2d, g_pre, w_in_perm, groups, out_defs)

    o_a, s_new = _gdn(qkv_a, z_a, ba, conv_w, conv_init, s0, a_log, dt_bias, g_gdn, B, L)
    if k_past is None:
        o_b = _diff_attn(q_b16, k_b16, v_b16, z_b, lam_vec, g_sub, B, L, lam_init)
    else:
        o_b = _diff_attn_kv(q_b16, k_past, v_past, k_b16, v_b16, z_b, lam_vec, g_sub, B, L, lam_init)
    o_c = _mem_attn(q_c16, mem_k, mem_v, z_c, B, L, n_mem)
    y = _out_proj(o_a, o_b, o_c, x2d, w_out_bf16, g_final, final_norm).reshape(B, L, D)
    conv_new = qkv_a.reshape(B, L, QKV_A)[:, L - (conv_w.shape[0] - 1):, :]
    return y, k_b, v_b, conv_new, s_new


def kernel(x_prompt, x_sample, mem_prompt, cache_diff_k, cache_diff_v, cache_mem_k, cache_mem_v,
           state_gdn, cache_conv, g_pre, w_in, conv_w, a_log, dt_bias, g_gdn, lam_vec, g_sub, g_mem,
           w_mem_kv, w_out, g_final):
    depth = w_in.shape[0]
    B, L, D = x_prompt.shape
    Bs, Ls, _ = x_sample.shape
    _, _, P, HB, _, DQK = cache_diff_k.shape
    DV_B = cache_diff_v.shape[-1]
    _, _, n_mem, HC, DH_C = cache_mem_k.shape
    _, _, HA, DK_A, DV_A = state_gdn.shape
    CW = conv_w.shape[1]
    QKV_A = conv_w.shape[2]
    WA, WB, WC = HA * DV_A, HB * DV_B, HC * DH_C
    assert QKV_A == HA * (2 * DK_A + DV_A) and DK_A == DV_A == LANES and DH_C == LANES
    in_sizes = (QKV_A, WA, HA, HA, HB * 2 * DQK, HB * 2 * DQK, WB, WB, WC, WC)
    assert w_in.shape[2] == sum(in_sizes) and 2 * HA <= LANES
    dims = (QKV_A, WA, WB, WC, n_mem)

    hp, hs = x_prompt, x_sample
    outs = [[] for _ in range(10)]
    for i in range(depth):
        lam_init = 0.8 - 0.6 * math.exp(-0.3 * i)
        final_norm = i == depth - 1
        qkv_w, za_w, b_w, a_w, qb_w, kb_w, vb_w, zb_w, qc_w, zc_w = _split_w_in(w_in[i], in_sizes)
        ba_w = jnp.concatenate([b_w, a_w, jnp.zeros((D, LANES - 2 * HA), w_in.dtype)], axis=1)
        w_in_perm = jnp.concatenate(
            [qkv_w, za_w, kb_w, vb_w, zb_w, zc_w, qb_w, qc_w, ba_w], axis=1).astype(BF16)
        w_out_bf16 = w_out[i].astype(BF16)

        mk, mk16, mv, mv16 = _norm_matmul(
            mem_prompt.reshape(B * n_mem, D), g_mem[i], w_mem_kv[i].astype(BF16),
            ((0, WC, (0, 1)), (WC, WC, (2, 3))), ((WC, F32), (WC, BF16), (WC, F32), (WC, BF16)))

        wts = (w_in_perm, conv_w[i], a_log[i], dt_bias[i], g_pre[i], g_gdn[i], lam_vec[i], g_sub[i],
               w_out_bf16, g_final, lam_init, final_norm, dims)
        hp, kp, vp, cp, sp = _layer(
            hp, None, None, mk16, mv16, jnp.zeros((B, CW - 1, QKV_A), F32),
            jnp.zeros((B, HA, DK_A, DV_A), F32), *wts)
        hs, kn, vn, cn, sn = _layer(
            hs, cache_diff_k[i].reshape(Bs, P, HB * 2 * DQK), cache_diff_v[i].reshape(Bs, P, WB),
            cache_mem_k[i].reshape(Bs * n_mem, WC), cache_mem_v[i].reshape(Bs * n_mem, WC),
            cache_conv[i], state_gdn[i], *wts)
        layer_out = (kp.reshape(B, L, HB, 2, DQK), vp.reshape(B, L, HB, DV_B),
                     mk.reshape(B, n_mem, HC, DH_C), mv.reshape(B, n_mem, HC, DH_C), sp, cp,
                     kn.reshape(Bs, Ls, HB, 2, DQK), vn.reshape(Bs, Ls, HB, DV_B), sn, cn)
        for lst, o in zip(outs, layer_out):
            lst.append(o)
    return (hp, hs) + tuple(jnp.stack(o) for o in outs)
```
